```python
import math
import jax, jax.numpy as jnp
from jax import lax
import numpy as np

D_MODEL = 1024
BATCH = 4
SEQ = 4096
DEPTH = 4
DEC_BATCH = 32
DEC_SEQ = 4
PAST_LEN = 8192
PAGE_SIZE = 128

N_A_LAYERS = DEPTH // 2
N_B_LAYERS = DEPTH - N_A_LAYERS
POOL_WINDOWS = (2, 4, 8, 16)
N_POOL_GROUPS = len(POOL_WINDOWS)
POOL_GROUP_DIM = D_MODEL // N_POOL_GROUPS
POOL_STATE = max(POOL_WINDOWS) - 1
HEAD_DIM = 128
N_HEADS = D_MODEL // HEAD_DIM
MOBA_BLOCK = 256
MOBA_TOP_K = 3
QUERY_BLOCK = 128
ALIBI_MAX_BIAS = 8.0
N_GROUPS = 4
EXPERTS_PER_GROUP = 4
N_EXPERTS = N_GROUPS * EXPERTS_PER_GROUP
EXPERT_TOP_K = 2
D_EXPERT = D_MODEL // 2
PLE_DIM = 256
NORM_EPS = 1e-6

kernel_name = 'yoco_pool_moba_hmoe_step'


def rmsnorm(x, g):
    xf = x.astype(jnp.float32)
    y = xf * lax.rsqrt(jnp.mean(xf * xf, axis=-1, keepdims=True) + NORM_EPS)
    return (y * g.astype(jnp.float32)).astype(x.dtype)


def alibi_slopes():
    return jnp.exp2(-ALIBI_MAX_BIAS * jnp.arange(1, N_HEADS + 1, dtype=jnp.float32) / N_HEADS)


def multiscale_pool(ext, pos, w_pool, pool_scale):
    bsz, n_ext, _ = ext.shape
    n_q = pos.shape[0]
    n_pre = n_ext - n_q
    xf = ext.astype(jnp.float32).reshape(bsz, n_ext, N_POOL_GROUPS, POOL_GROUP_DIM)
    cs = jnp.concatenate([jnp.zeros_like(xf[:, :1]), jnp.cumsum(xf, axis=1)], axis=1)
    hi = cs[:, n_pre + 1:]
    pooled = []
    for g, w in enumerate(POOL_WINDOWS):
        lo = cs[:, n_pre + 1 - w: n_pre + 1 - w + n_q, g]
        cnt = jnp.minimum(pos + 1, w).astype(jnp.float32)
        pooled.append((hi[:, :, g] - lo) / cnt[None, :, None])
    diff = jnp.stack(pooled, axis=2) - xf[:, n_pre:]
    y = jnp.einsum('bqgc,gcd->bqgd', diff, w_pool.astype(jnp.float32)).reshape(bsz, n_q, D_MODEL)
    return (y * pool_scale.astype(jnp.float32)).astype(ext.dtype)


def moba_attention(q, q_pos, k, v):
    bsz, n_q, n_h, d_h = q.shape
    n_k = k.shape[1]
    n_blk = -(-n_k // MOBA_BLOCK)
    pad = n_blk * MOBA_BLOCK - n_k

    def to_blocks(t):
        t = jnp.pad(t, ((0, 0), (0, pad), (0, 0), (0, 0)))
        return t.reshape(bsz, n_blk, MOBA_BLOCK, n_h, d_h).transpose(0, 3, 1, 2, 4)

    kb, vb = to_blocks(k), to_blocks(v)
    k_mean = jnp.mean(kb.astype(jnp.float32), axis=3)
    n_sel = min(MOBA_TOP_K, n_blk)
    slopes = alibi_slopes()
    scale = HEAD_DIM ** -0.5
    b_idx = jnp.arange(bsz)[:, None, None, None]
    h_idx = jnp.arange(n_h)[None, :, None, None]
    blk_off = jnp.arange(MOBA_BLOCK)

    def attend(args):
        qc, pc = args
        n_c = pc.shape[0]
        qh = qc.transpose(0, 2, 1, 3)
        own = pc // MOBA_BLOCK
        gate = jnp.einsum('bhqd,bhnd->bhqn', qh.astype(jnp.float32), k_mean)
        fully_past = jnp.arange(n_blk)[None, :] < own[:, None]
        gate = jnp.where(fully_past, gate, -jnp.inf)
        _, sel = lax.top_k(gate, n_sel)
        sel_ok = sel < own[:, None]
        own_b = jnp.broadcast_to(own[None, None, :, None], (bsz, n_h, n_c, 1)).astype(sel.dtype)
        blk = jnp.concatenate([sel, own_b], axis=-1)
        blk_ok = jnp.concatenate([sel_ok, jnp.ones_like(sel_ok[..., :1])], axis=-1)
        kg = kb[b_idx, h_idx, blk]
        vg = vb[b_idx, h_idx, blk]
        s = jnp.einsum('bhqd,bhqjsd->bhqjs', qh, kg).astype(jnp.float32) * scale
        dist = (pc[None, None, :, None, None] - (blk[..., None] * MOBA_BLOCK + blk_off)).astype(jnp.float32)
        s = jnp.where(blk_ok[..., None] & (dist >= 0), s - slopes[None, :, None, None, None] * dist, -jnp.inf)
        pr = jax.nn.softmax(s.reshape(bsz, n_h, n_c, -1), axis=-1).reshape(s.shape)
        return jnp.einsum('bhqjs,bhqjsd->bqhd', pr.astype(vg.dtype), vg)

    qb = QUERY_BLOCK if n_q % QUERY_BLOCK == 0 else n_q
    n_qb = n_q // qb
    qs = q.reshape(bsz, n_qb, qb, n_h, d_h).transpose(1, 0, 2, 3, 4)
    ps = q_pos.reshape(n_qb, qb)
    out = lax.map(attend, (qs, ps))
    return out.transpose(1, 0, 2, 3, 4).reshape(bsz, n_q, n_h, d_h)


def hier_moe(xn, wr_g, br_g, wr_e, br_e, w_gate, w_up, w_down):
    bsz, n_s, _ = xn.shape
    xf = xn.astype(jnp.float32)
    lg = xf @ wr_g.astype(jnp.float32) + br_g.astype(jnp.float32)
    pg = jax.nn.softmax(lg, axis=-1)
    g_idx = jnp.argmax(lg, axis=-1)
    p_grp = jnp.take_along_axis(pg, g_idx[..., None], axis=-1)
    le = (xf @ wr_e.astype(jnp.float32) + br_e.astype(jnp.float32)).reshape(bsz, n_s, N_GROUPS, EXPERTS_PER_GROUP)
    le_sel = jnp.take_along_axis(le, g_idx[..., None, None], axis=2)[:, :, 0]
    top_v, top_i = lax.top_k(le_sel, EXPERT_TOP_K)
    top_w = jax.nn.softmax(top_v, axis=-1) * p_grp
    w_in = jnp.sum(jax.nn.one_hot(top_i, EXPERTS_PER_GROUP, dtype=jnp.float32) * top_w[..., None], axis=-2)
    gates = jax.nn.one_hot(g_idx, N_GROUPS, dtype=jnp.float32)[..., None] * w_in[:, :, None, :]
    y = jnp.zeros((bsz, n_s, D_MODEL), jnp.float32)
    for g in range(N_GROUPS):
        sl = slice(g * EXPERTS_PER_GROUP, (g + 1) * EXPERTS_PER_GROUP)
        a = jnp.einsum('bsd,edf->bsef', xn, w_gate[sl])
        u = jnp.einsum('bsd,edf->bsef', xn, w_up[sl])
        h = jax.nn.silu(a) * u * gates[:, :, g, :, None].astype(xn.dtype)
        y = y + jnp.einsum('bsef,efd->bsd', h, w_down[sl]).astype(jnp.float32)
    return y.astype(xn.dtype)


def shared_kv(h, norm_kv, w_kv, k_norm):
    bsz, n_q, _ = h.shape
    kv = (rmsnorm(h, norm_kv) @ w_kv).reshape(bsz, n_q, 2, N_HEADS, HEAD_DIM)
    return rmsnorm(kv[:, :, 0], k_norm), kv[:, :, 1]


def trunk(x, p, pool_prefix, pos, k_past, v_past, w):
    h = x
    new_pool = []
    k_new = v_new = k_all = v_all = None
    for l in range(DEPTH):
        hn = rmsnorm(h, w['norm_mix'][l])
        if l < N_A_LAYERS:
            ext = jnp.concatenate([pool_prefix[l].astype(hn.dtype), hn], axis=1)
            new_pool.append(ext[:, -POOL_STATE:])
            h = h + multiscale_pool(ext, pos, w['w_pool'][l], w['pool_scale'][l])
        else:
            j = l - N_A_LAYERS
            bsz, n_q, _ = hn.shape
            q = rmsnorm((hn @ w['w_q'][j]).reshape(bsz, n_q, N_HEADS, HEAD_DIM), w['q_norm'][j])
            o = moba_attention(q, pos, k_all, v_all)
            h = h + o.reshape(bsz, n_q, N_HEADS * HEAD_DIM) @ w['w_o'][j]
        h = h + hier_moe(rmsnorm(h, w['norm_ffn'][l]), w['router_group_w'][l], w['router_group_b'][l],
                         w['router_expert_w'][l], w['router_expert_b'][l],
                         w['w_gate'][l], w['w_up'][l], w['w_down'][l])
        gate = jax.nn.sigmoid((rmsnorm(h, w['norm_ple'][l]) @ w['w_ple_gate'][l]).astype(jnp.float32)).astype(h.dtype)
        h = h + (p[l] @ w['w_ple'][l]) * gate
        if l == N_A_LAYERS - 1:
            k_new, v_new = shared_kv(h, w['norm_kv'], w['w_kv'], w['k_norm'])
            if k_past is None:
                k_all, v_all = k_new, v_new
            else:
                k_all = jnp.concatenate([k_past.astype(k_new.dtype), k_new], axis=1)
                v_all = jnp.concatenate([v_past.astype(v_new.dtype), v_new], axis=1)
    return h, jnp.stack(new_pool), k_new, v_new


def setup_inputs(seed: int = 0) -> dict:
    key = jax.random.key(seed)
    keys = jax.random.split(key, 32)
    counter = [0]

    def nxt():
        counter[0] += 1
        return keys[counter[0] - 1]

    def nrm(shape, scale=1.0):
        return jax.random.normal(nxt(), shape, jnp.float32) * scale

    def gain(shape):
        return 1.0 + nrm(shape, 0.05)

    n_pages = PAST_LEN // PAGE_SIZE
    n_used = DEC_BATCH * n_pages
    n_phys = n_used + max(1, n_used // 4)
    qkv = N_HEADS * HEAD_DIM
    x_prompt = nrm((BATCH, SEQ, D_MODEL))
    x_sample = nrm((DEC_BATCH, DEC_SEQ, D_MODEL))
    p_prompt = nrm((DEPTH, BATCH, SEQ, PLE_DIM))
    p_sample = nrm((DEPTH, DEC_BATCH, DEC_SEQ, PLE_DIM))
    state_pool = nrm((N_A_LAYERS, DEC_BATCH, POOL_STATE, D_MODEL))
    cache_k = nrm((n_phys, PAGE_SIZE, N_HEADS, HEAD_DIM))
    cache_v = nrm((n_phys, PAGE_SIZE, N_HEADS, HEAD_DIM))
    page_table = jax.random.permutation(nxt(), n_phys)[:n_used].reshape(DEC_BATCH, n_pages).astype(jnp.int32)
    return {
        'x_prompt': x_prompt,
        'x_sample': x_sample,
        'p_prompt': p_prompt,
        'p_sample': p_sample,
        'state_pool': state_pool,
        'cache_k': cache_k,
        'cache_v': cache_v,
        'page_table': page_table,
        'norm_mix': gain((DEPTH, D_MODEL)),
        'norm_ffn': gain((DEPTH, D_MODEL)),
        'norm_ple': gain((DEPTH, D_MODEL)),
        'w_pool': nrm((N_A_LAYERS, N_POOL_GROUPS, POOL_GROUP_DIM, POOL_GROUP_DIM), POOL_GROUP_DIM ** -0.5),
        'pool_scale': gain((N_A_LAYERS, D_MODEL)),
        'norm_kv': gain((D_MODEL,)),
        'w_kv': nrm((D_MODEL, 2 * qkv), D_MODEL ** -0.5),
        'k_norm': gain((HEAD_DIM,)),
        'w_q': nrm((N_B_LAYERS, D_MODEL, qkv), D_MODEL ** -0.5),
        'q_norm': gain((N_B_LAYERS, HEAD_DIM)),
        'w_o': nrm((N_B_LAYERS, qkv, D_MODEL), 0.5 * qkv ** -0.5),
        'router_group_w': nrm((DEPTH, D_MODEL, N_GROUPS), D_MODEL ** -0.5),
        'router_group_b': nrm((DEPTH, N_GROUPS), 0.01),
        'router_expert_w': nrm((DEPTH, D_MODEL, N_EXPERTS), D_MODEL ** -0.5),
        'router_expert_b': nrm((DEPTH, N_EXPERTS), 0.01),
        'w_gate': nrm((DEPTH, N_EXPERTS, D_MODEL, D_EXPERT), D_MODEL ** -0.5),
        'w_up': nrm((DEPTH, N_EXPERTS, D_MODEL, D_EXPERT), D_MODEL ** -0.5),
        'w_down': nrm((DEPTH, N_EXPERTS, D_EXPERT, D_MODEL), D_EXPERT ** -0.5),
        'w_ple': nrm((DEPTH, PLE_DIM, D_MODEL), 0.5 * PLE_DIM ** -0.5),
        'w_ple_gate': nrm((DEPTH, D_MODEL, D_MODEL), D_MODEL ** -0.5),
    }


def reference(x_prompt, x_sample, p_prompt, p_sample, state_pool, cache_k, cache_v, page_table,
              norm_mix, norm_ffn, norm_ple, w_pool, pool_scale, norm_kv, w_kv, k_norm, w_q, q_norm, w_o,
              router_group_w, router_group_b, router_expert_w, router_expert_b, w_gate, w_up, w_down,
              w_ple, w_ple_gate):
    w = dict(norm_mix=norm_mix, norm_ffn=norm_ffn, norm_ple=norm_ple, w_pool=w_pool, pool_scale=pool_scale,
             norm_kv=norm_kv, w_kv=w_kv, k_norm=k_norm, w_q=w_q, q_norm=q_norm, w_o=w_o,
             router_group_w=router_group_w, router_group_b=router_group_b,
             router_expert_w=router_expert_w, router_expert_b=router_expert_b,
             w_gate=w_gate, w_up=w_up, w_down=w_down, w_ple=w_ple, w_ple_gate=w_ple_gate)
    bsz, seq, _ = x_prompt.shape
    pos_prompt = jnp.arange(seq, dtype=jnp.int32)
    pool_zero = jnp.zeros((N_A_LAYERS, bsz, POOL_STATE, D_MODEL), x_prompt.dtype)
    y_prompt, pool_prompt, k_prompt, v_prompt = trunk(x_prompt, p_prompt, pool_zero, pos_prompt, None, None, w)
    dec_b, n_pages = page_table.shape
    past_len = n_pages * cache_k.shape[1]
    k_past = cache_k[page_table].reshape(dec_b, past_len, N_HEADS, HEAD_DIM)
    v_past = cache_v[page_table].reshape(dec_b, past_len, N_HEADS, HEAD_DIM)
    pos_sample = past_len + jnp.arange(x_sample.shape[1], dtype=jnp.int32)
    y_sample, pool_sample, k_sample, v_sample = trunk(x_sample, p_sample, state_pool, pos_sample, k_past, v_past, w)
    return (y_prompt, y_sample, pool_prompt, pool_sample, k_prompt, v_prompt, k_sample, v_sample)
```

```python
import functools

import jax
import jax.numpy as jnp
from jax import lax
from jax.experimental import pallas as pl
from jax.experimental.pallas import tpu as pltpu

F32 = jnp.float32
BF16 = jnp.bfloat16

D_MODEL = 1024
DEPTH = 4
N_A_LAYERS = DEPTH // 2
POOL_WINDOWS = (2, 4, 8, 16)
POOL_GROUP_DIM = D_MODEL // len(POOL_WINDOWS)
POOL_STATE = max(POOL_WINDOWS) - 1
POOL_HALO = POOL_STATE + 1
HEAD_DIM = 128
N_HEADS = D_MODEL // HEAD_DIM
MOBA_BLOCK = 256
MOBA_TOP_K = 3
PAGE_SIZE = 128
PAGES_PER_BLOCK = MOBA_BLOCK // PAGE_SIZE
ALIBI_MAX_BIAS = 8.0
N_GROUPS = 4
EXPERTS_PER_GROUP = 4
N_EXPERTS = N_GROUPS * EXPERTS_PER_GROUP
D_EXPERT = D_MODEL // 2
PLE_DIM = 256
NORM_EPS = 1e-6
LANES = 128
SAMPLE_ROWS = 8
NEG = -1e30
VMEM_LIMIT = 48 * 1024 * 1024


def _cparams(*sem):
    return pltpu.CompilerParams(dimension_semantics=sem, vmem_limit_bytes=VMEM_LIMIT)


def _rms(x, g):
    return x * lax.rsqrt(jnp.mean(x * x, axis=-1, keepdims=True) + NORM_EPS) * g


def _dot(a, b):
    return jnp.dot(a, b, preferred_element_type=F32)


def _split(x):
    bits = lax.bitcast_convert_type(x, jnp.uint32) & jnp.uint32(0xFFFF0000)
    hi = lax.bitcast_convert_type(bits, F32)
    return hi.astype(BF16), (x - hi).astype(BF16)


def _dot3(a, b_hi, b_lo):
    a_hi, a_lo = _split(a)
    return _dot(a_hi, b_hi) + (_dot(a_hi, b_lo) + _dot(a_lo, b_hi))


def _sigmoid(x):
    return 1.0 / (1.0 + jnp.exp(-x))


def _pool_kernel(h_ref, pre_ref, g_ref, whi_ref, wlo_ref, sc_ref, o_ref, st_ref, ext_ref, *, ts, n_valid, pos0):
    s = pl.program_id(1)

    @pl.when(s == 0)
    def _():
        ext_ref[0:POOL_HALO, :] = pre_ref[0]

    x = h_ref[0]
    hn = _rms(x, g_ref[...])
    ext_ref[POOL_HALO:POOL_HALO + ts, :] = hn
    pos = pos0 + s * ts + lax.broadcasted_iota(jnp.int32, (ts, 1), 0)
    for g, w in enumerate(POOL_WINDOWS):
        c0, c1 = g * POOL_GROUP_DIM, (g + 1) * POOL_GROUP_DIM
        acc = hn[:, c0:c1]
        for j in range(1, w):
            acc = acc + ext_ref[POOL_HALO - j:POOL_HALO - j + ts, c0:c1]
        cnt = jnp.minimum(pos + 1, w).astype(F32)
        diff = acc / cnt - hn[:, c0:c1]
        y = _dot3(diff, whi_ref[g], wlo_ref[g])
        o_ref[0, :, c0:c1] = x[:, c0:c1] + y * sc_ref[:, c0:c1]
    tail = ext_ref[n_valid:n_valid + POOL_HALO, :]
    st_ref[0] = tail
    ext_ref[0:POOL_HALO, :] = tail


def _pool_mixer(h, prefix, g, w_hi, w_lo, scale, *, ts, n_valid, pos0):
    bsz, seq, _ = h.shape
    kern = functools.partial(_pool_kernel, ts=ts, n_valid=n_valid, pos0=pos0)
    return pl.pallas_call(
        kern,
        out_shape=(jax.ShapeDtypeStruct(h.shape, F32), jax.ShapeDtypeStruct((bsz, POOL_HALO, D_MODEL), F32)),
        grid=(bsz, seq // ts),
        in_specs=[
            pl.BlockSpec((1, ts, D_MODEL), lambda b, s: (b, s, 0)),
            pl.BlockSpec((1, POOL_HALO, D_MODEL), lambda b, s: (b, 0, 0)),
            pl.BlockSpec((1, D_MODEL), lambda b, s: (0, 0)),
            pl.BlockSpec((len(POOL_WINDOWS), POOL_GROUP_DIM, POOL_GROUP_DIM), lambda b, s: (0, 0, 0)),
            pl.BlockSpec((len(POOL_WINDOWS), POOL_GROUP_DIM, POOL_GROUP_DIM), lambda b, s: (0, 0, 0)),
            pl.BlockSpec((1, D_MODEL), lambda b, s: (0, 0)),
        ],
        out_specs=(
            pl.BlockSpec((1, ts, D_MODEL), lambda b, s: (b, s, 0)),
            pl.BlockSpec((1, POOL_HALO, D_MODEL), lambda b, s: (b, 0, 0)),
        ),
        scratch_shapes=[pltpu.VMEM((POOL_HALO + ts, D_MODEL), F32)],
        compiler_params=_cparams("parallel", "arbitrary"),
        name="pool_mixer",
    )(h, prefix, g, w_hi, w_lo, scale)


def _route(xn, wr_hi, wr_lo, br):
    logits = _dot3(xn, wr_hi, wr_lo) + br
    lane = lax.broadcasted_iota(jnp.int32, logits.shape, 1)
    big = jnp.int32(1 << 20)
    lg = jnp.where(lane < N_GROUPS, logits, -jnp.inf)
    mg = jnp.max(lg, axis=-1, keepdims=True)
    g_idx = jnp.min(jnp.where(lg == mg, lane, big), axis=-1, keepdims=True)
    p_grp = 1.0 / jnp.sum(jnp.exp(lg - mg), axis=-1, keepdims=True)
    e0 = N_GROUPS + EXPERTS_PER_GROUP * g_idx
    le = jnp.where((lane >= e0) & (lane < e0 + EXPERTS_PER_GROUP), logits, -jnp.inf)
    m1 = jnp.max(le, axis=-1, keepdims=True)
    i1 = jnp.min(jnp.where(le == m1, lane, big), axis=-1, keepdims=True)
    le2 = jnp.where(lane == i1, -jnp.inf, le)
    m2 = jnp.max(le2, axis=-1, keepdims=True)
    i2 = jnp.min(jnp.where(le2 == m2, lane, big), axis=-1, keepdims=True)
    e2 = jnp.exp(m2 - m1)
    w1 = p_grp / (1.0 + e2)
    w2 = p_grp * e2 / (1.0 + e2)
    return jnp.where(lane == i1, w1, 0.0) + jnp.where(lane == i2, w2, 0.0)


def _moe_kernel(h_ref, g_ref, wrh_ref, wrl_ref, br_ref, wg_ref, wu_ref, wd_ref, o_ref, xn_ref, gates_ref, acc_ref):
    e = pl.program_id(1)

    @pl.when(e == 0)
    def _():
        xn = _rms(h_ref[...], g_ref[...])
        xn_ref[...] = xn.astype(BF16)
        gates_ref[...] = _route(xn, wrh_ref[...], wrl_ref[...], br_ref[...])
        acc_ref[...] = jnp.zeros_like(acc_ref)

    xb = xn_ref[...]
    a = _dot(xb, wg_ref[...])
    u = _dot(xb, wu_ref[...])
    gates = gates_ref[...]
    lane = lax.broadcasted_iota(jnp.int32, gates.shape, 1)
    gate = jnp.sum(jnp.where(lane == e + N_GROUPS, gates, 0.0), axis=-1, keepdims=True)
    hid = a * _sigmoid(a) * u * gate
    acc_ref[...] += _dot(hid.astype(BF16), wd_ref[...])

    @pl.when(e == N_EXPERTS - 1)
    def _():
        o_ref[...] = h_ref[...] + acc_ref[...]


def _moe(h, l, w, *, tm):
    n = h.shape[0]
    return pl.pallas_call(
        _moe_kernel,
        out_shape=jax.ShapeDtypeStruct(h.shape, F32),
        grid=(n // tm, N_EXPERTS),
        in_specs=[
            pl.BlockSpec((tm, D_MODEL), lambda i, e: (i, 0)),
            pl.BlockSpec((None, 1, D_MODEL), lambda i, e: (l, 0, 0)),
            pl.BlockSpec((None, D_MODEL, LANES), lambda i, e: (l, 0, 0)),
            pl.BlockSpec((None, D_MODEL, LANES), lambda i, e: (l, 0, 0)),
            pl.BlockSpec((None, 1, LANES), lambda i, e: (l, 0, 0)),
            pl.BlockSpec((None, None, D_MODEL, D_EXPERT), lambda i, e: (l, e, 0, 0)),
            pl.BlockSpec((None, None, D_MODEL, D_EXPERT), lambda i, e: (l, e, 0, 0)),
            pl.BlockSpec((None, None, D_EXPERT, D_MODEL), lambda i, e: (l, e, 0, 0)),
        ],
        out_specs=pl.BlockSpec((tm, D_MODEL), lambda i, e: (i, 0)),
        scratch_shapes=[
            pltpu.VMEM((tm, D_MODEL), BF16),
            pltpu.VMEM((tm, LANES), F32),
            pltpu.VMEM((tm, D_MODEL), F32),
        ],
        compiler_params=_cparams("parallel", "arbitrary"),
        name="hier_moe",
    )(h, w["norm_ffn"], w["wr_hi"], w["wr_lo"], w["br"], w["w_gate"], w["w_up"], w["w_down"])


def _ple_kernel(h_ref, p_ref, g_ref, wg_ref, wp_ref, o_ref):
    h = h_ref[...]
    hn = _rms(h, g_ref[...]).astype(BF16)
    gate = _sigmoid(_dot(hn, wg_ref[...]))
    o_ref[...] = h + _dot(p_ref[...].astype(BF16), wp_ref[...]) * gate


def _ple(h, p, l, w, *, tm):
    n = h.shape[0]
    return pl.pallas_call(
        _ple_kernel,
        out_shape=jax.ShapeDtypeStruct(h.shape, F32),
        grid=(n // tm,),
        in_specs=[
            pl.BlockSpec((tm, D_MODEL), lambda i: (i, 0)),
            pl.BlockSpec((None, tm, PLE_DIM), lambda i: (l, i, 0)),
            pl.BlockSpec((None, 1, D_MODEL), lambda i: (l, 0, 0)),
            pl.BlockSpec((None, D_MODEL, D_MODEL), lambda i: (l, 0, 0)),
            pl.BlockSpec((None, PLE_DIM, D_MODEL), lambda i: (l, 0, 0)),
        ],
        out_specs=pl.BlockSpec((tm, D_MODEL), lambda i: (i, 0)),
        compiler_params=_cparams("parallel"),
        name="ple",
    )(h, p, w["norm_ple"], w["w_ple_gate"], w["w_ple"])


def _head_rms(x, g):
    outs = []
    for hh in range(N_HEADS):
        outs.append(_rms(x[:, hh * HEAD_DIM:(hh + 1) * HEAD_DIM], g))
    return jnp.concatenate(outs, axis=-1)


def _kv_kernel(h_ref, g_ref, w_ref, kn_ref, k_ref, v_ref, kb_ref, vb_ref, km_ref):
    hn = _rms(h_ref[...], g_ref[...]).astype(BF16)
    kv = _dot(hn, w_ref[...])
    k = _head_rms(kv[:, :D_MODEL], kn_ref[...])
    v = kv[:, D_MODEL:]
    k_ref[...] = k
    v_ref[...] = v
    kb_ref[...] = k.astype(BF16)
    vb_ref[...] = v.astype(BF16)
    km_ref[0] = jnp.sum(k, axis=0, keepdims=True) / MOBA_BLOCK


def _shared_kv(h, w, *, tm):
    n = h.shape[0]
    tok = pl.BlockSpec((tm, D_MODEL), lambda i: (i, 0))
    return pl.pallas_call(
        _kv_kernel,
        out_shape=(
            jax.ShapeDtypeStruct((n, D_MODEL), F32), jax.ShapeDtypeStruct((n, D_MODEL), F32),
            jax.ShapeDtypeStruct((n, D_MODEL), BF16), jax.ShapeDtypeStruct((n, D_MODEL), BF16),
            jax.ShapeDtypeStruct((n // tm, 1, D_MODEL), F32),
        ),
        grid=(n // tm,),
        in_specs=[
            tok,
            pl.BlockSpec((1, D_MODEL), lambda i: (0, 0)),
            pl.BlockSpec((D_MODEL, 2 * D_MODEL), lambda i: (0, 0)),
            pl.BlockSpec((1, HEAD_DIM), lambda i: (0, 0)),
        ],
        out_specs=(tok, tok, tok, tok, pl.BlockSpec((1, 1, D_MODEL), lambda i: (i, 0, 0))),
        compiler_params=_cparams("parallel"),
        name="shared_kv",
    )(h, w["norm_kv"], w["w_kv"], w["k_norm"])


def _rank_in_segments(gv, seg):
    n_lanes = gv.shape[-1]
    lane = lax.broadcasted_iota(jnp.int32, gv.shape, 1)
    n = lane & (seg - 1)
    rank = jnp.zeros(gv.shape, jnp.int32)
    for k in range(1, seg):
        ahead = pltpu.roll(gv, n_lanes - k, axis=1)
        behind = pltpu.roll(gv, seg - k, axis=1)
        wrap = n + k >= seg
        partner = jnp.where(wrap, behind, ahead)
        rank = rank + jnp.where(wrap, (partner >= gv).astype(jnp.int32), (partner > gv).astype(jnp.int32))
    return rank


def _queries(h_ref, g_ref, wq_ref, qn_ref):
    hn = _rms(h_ref[...], g_ref[...]).astype(BF16)
    return _head_rms(_dot(hn, wq_ref[...]), qn_ref[...])


def _q_prompt_kernel(h_ref, g_ref, wq_ref, qn_ref, kmh_ref, kml_ref, q_ref, bias_ref, *, blocks_per_seq):
    own = pl.program_id(0) % blocks_per_seq
    q = _queries(h_ref, g_ref, wq_ref, qn_ref)
    q_ref[...] = (q * HEAD_DIM ** -0.5).astype(BF16)
    gate = _dot3(q, kmh_ref[0], kml_ref[0])
    lane = lax.broadcasted_iota(jnp.int32, gate.shape, 1)
    valid = (lane & (blocks_per_seq - 1)) < own
    gv = jnp.where(valid, gate, -jnp.inf)
    rank = _rank_in_segments(gv, blocks_per_seq)
    bias_ref[...] = jnp.where(valid & (rank < MOBA_TOP_K), 0.0, NEG)


def _q_prompt(h, j, w, km_hi, km_lo, *, blocks_per_seq):
    n = h.shape[0]
    tm = MOBA_BLOCK
    kern = functools.partial(_q_prompt_kernel, blocks_per_seq=blocks_per_seq)
    return pl.pallas_call(
        kern,
        out_shape=(jax.ShapeDtypeStruct((n, D_MODEL), BF16), jax.ShapeDtypeStruct((n, LANES), F32)),
        grid=(n // tm,),
        in_specs=[
            pl.BlockSpec((tm, D_MODEL), lambda i: (i, 0)),
            pl.BlockSpec((None, 1, D_MODEL), lambda i: (N_A_LAYERS + j, 0, 0)),
            pl.BlockSpec((None, D_MODEL, D_MODEL), lambda i: (j, 0, 0)),
            pl.BlockSpec((None, 1, HEAD_DIM), lambda i: (j, 0, 0)),
            pl.BlockSpec((1, D_MODEL, LANES), lambda i: (i // blocks_per_seq, 0, 0)),
            pl.BlockSpec((1, D_MODEL, LANES), lambda i: (i // blocks_per_seq, 0, 0)),
        ],
        out_specs=(pl.BlockSpec((tm, D_MODEL), lambda i: (i, 0)), pl.BlockSpec((tm, LANES), lambda i: (i, 0))),
        compiler_params=_cparams("parallel"),
        name="q_prompt",
    )(h, w["norm_mix"], w["w_q"], w["q_norm"], km_hi, km_lo)


def _q_sample_kernel(h_ref, g_ref, wq_ref, qn_ref, kmh_ref, kml_ref, q_ref, idx_ref, *, n_blk):
    q = _queries(h_ref, g_ref, wq_ref, qn_ref)
    q_ref[...] = q * HEAD_DIM ** -0.5
    gate = _dot3(q, kmh_ref[0], kml_ref[0])
    rank = _rank_in_segments(gate, n_blk)
    lane = lax.broadcasted_iota(jnp.int32, gate.shape, 1)
    blk = (lane & (n_blk - 1)).astype(F32)
    row = lax.broadcasted_iota(jnp.int32, (gate.shape[1], LANES), 0)
    col = lax.broadcasted_iota(jnp.int32, (gate.shape[1], LANES), 1)
    seg_of = jnp.where((row >= col * n_blk) & (row < (col + 1) * n_blk), 1.0, 0.0).astype(BF16)
    for r in range(MOBA_TOP_K):
        picked = jnp.where(rank == r, blk, 0.0).astype(BF16)
        idx_ref[r] = _dot(picked, seg_of).astype(jnp.int32)


def _q_sample(h, j, w, km_hi, km_lo, *, n_blk):
    n = h.shape[0]
    tm = SAMPLE_ROWS
    kern = functools.partial(_q_sample_kernel, n_blk=n_blk)
    return pl.pallas_call(
        kern,
        out_shape=(jax.ShapeDtypeStruct((n, D_MODEL), F32), jax.ShapeDtypeStruct((MOBA_TOP_K, n, LANES), jnp.int32)),
        grid=(n // tm,),
        in_specs=[
            pl.BlockSpec((tm, D_MODEL), lambda i: (i, 0)),
            pl.BlockSpec((None, 1, D_MODEL), lambda i: (N_A_LAYERS + j, 0, 0)),
            pl.BlockSpec((None, D_MODEL, D_MODEL), lambda i: (j, 0, 0)),
            pl.BlockSpec((None, 1, HEAD_DIM), lambda i: (j, 0, 0)),
            pl.BlockSpec((1, D_MODEL, N_HEADS * n_blk), lambda i: (i, 0, 0)),
            pl.BlockSpec((1, D_MODEL, N_HEADS * n_blk), lambda i: (i, 0, 0)),
        ],
        out_specs=(pl.BlockSpec((tm, D_MODEL), lambda i: (i, 0)), pl.BlockSpec((MOBA_TOP_K, tm, LANES), lambda i: (0, i, 0))),
        compiler_params=_cparams("parallel"),
        name="q_sample",
    )(h, w["norm_mix"], w["w_q"], w["q_norm"], km_hi, km_lo)


def _block_diag_means(km, n_blk):
    bsz = km.shape[0]
    eye = jnp.eye(N_HEADS, dtype=F32)
    t = km.transpose(0, 2, 3, 1)[:, :, :, None, :] * eye[None, :, None, :, None]
    return _split(t.reshape(bsz, D_MODEL, N_HEADS * n_blk))


def _attn_prompt_kernel(q_ref, k_ref, v_ref, bias_ref, slope_ref, o_ref, *, blocks_per_seq):
    hh = pl.program_id(1)
    i = pl.program_id(2)
    blk = MOBA_BLOCK
    q = q_ref[...]
    slope = slope_ref[0]
    rows = lax.broadcasted_iota(jnp.int32, (blk, blk), 0)
    cols = lax.broadcasted_iota(jnp.int32, (blk, blk), 1)
    off = (rows - cols).astype(F32)
    slope_sq = jnp.concatenate([slope, slope], axis=-1)
    alibi = -slope_sq * off
    nt = (((1,), (1,)), ((), ()))

    kd = k_ref[pl.ds(pl.multiple_of(i * blk, blk), blk), :]
    vd = v_ref[pl.ds(pl.multiple_of(i * blk, blk), blk), :]
    s = lax.dot_general(q, kd, nt, preferred_element_type=F32) + alibi
    s = jnp.where(off >= 0, s, NEG)
    m = jnp.max(s, axis=-1, keepdims=True)
    p = jnp.exp(s - m)
    l = jnp.sum(p, axis=-1, keepdims=True)
    acc = _dot(p.astype(BF16), vd)

    bias_tile = bias_ref[...]
    lane = lax.broadcasted_iota(jnp.int32, bias_tile.shape, 1)
    slope_col = slope[:, 0:1]

    def body(n, carry):
        m, l, acc = carry
        kn = k_ref[pl.ds(pl.multiple_of(n * blk, blk), blk), :]
        vn = v_ref[pl.ds(pl.multiple_of(n * blk, blk), blk), :]
        chosen = jnp.sum(jnp.where(lane == hh * blocks_per_seq + n, bias_tile, 0.0), axis=-1, keepdims=True)
        shift = chosen - slope_col * ((i - n) * blk).astype(F32)
        s = lax.dot_general(q, kn, nt, preferred_element_type=F32) + alibi + shift
        m_new = jnp.maximum(m, jnp.max(s, axis=-1, keepdims=True))
        alpha = jnp.exp(m - m_new)
        p = jnp.exp(s - m_new)
        l = alpha * l + jnp.sum(p, axis=-1, keepdims=True)
        acc = alpha * acc + _dot(p.astype(BF16), vn)
        return m_new, l, acc

    m, l, acc = lax.fori_loop(0, i, body, (m, l, acc))
    o_ref[...] = (acc / l).astype(BF16)


def _attn_prompt(q, kb, vb, bias, slopes, *, bsz, blocks_per_seq):
    n = q.shape[0]
    seq = blocks_per_seq * MOBA_BLOCK
    kern = functools.partial(_attn_prompt_kernel, blocks_per_seq=blocks_per_seq)
    qspec = pl.BlockSpec((MOBA_BLOCK, HEAD_DIM), lambda b, h, i: (b * blocks_per_seq + i, h))
    kspec = pl.BlockSpec((seq, HEAD_DIM), lambda b, h, i: (b, h))
    return pl.pallas_call(
        kern,
        out_shape=jax.ShapeDtypeStruct((n, D_MODEL), BF16),
        grid=(bsz, N_HEADS, blocks_per_seq),
        in_specs=[
            qspec, kspec, kspec,
            pl.BlockSpec((MOBA_BLOCK, LANES), lambda b, h, i: (b * blocks_per_seq + i, 0)),
            pl.BlockSpec((1, 1, LANES), lambda b, h, i: (h, 0, 0)),
        ],
        out_specs=qspec,
        compiler_params=_cparams("parallel", "parallel", "arbitrary"),
        name="attn_prompt",
    )(q, kb, vb, bias, slopes)


def _kmean_cache_kernel(pt_ref, *refs):
    page_refs, o_ref = refs[:-1], refs[-1]
    for r in range(len(page_refs) // PAGES_PER_BLOCK):
        tot = jnp.sum(page_refs[2 * r][0], axis=0) + jnp.sum(page_refs[2 * r + 1][0], axis=0)
        o_ref[0, r] = tot / MOBA_BLOCK


def _kmean_cache(cache_k, page_table, *, pages_per_step=8):
    bsz, n_pages = page_table.shape
    steps = n_pages // pages_per_step
    blk_per_step = pages_per_step // PAGES_PER_BLOCK
    pt = page_table.reshape(-1)

    def page_spec(r):
        return pl.BlockSpec((1, PAGE_SIZE, N_HEADS, HEAD_DIM),
                            lambda b, j, pt_ref: (pt_ref[b * n_pages + j * pages_per_step + r], 0, 0, 0))

    out = pl.pallas_call(
        _kmean_cache_kernel,
        out_shape=jax.ShapeDtypeStruct((bsz * steps, blk_per_step, N_HEADS, HEAD_DIM), F32),
        grid_spec=pltpu.PrefetchScalarGridSpec(
            num_scalar_prefetch=1,
            grid=(bsz, steps),
            in_specs=[page_spec(r) for r in range(pages_per_step)],
            out_specs=pl.BlockSpec((1, blk_per_step, N_HEADS, HEAD_DIM), lambda b, j, pt_ref: (b * steps + j, 0, 0, 0)),
        ),
        compiler_params=_cparams("parallel", "arbitrary"),
        name="kmean_cache",
    )(pt, *([cache_k] * pages_per_step))
    return out.reshape(bsz, n_pages // PAGES_PER_BLOCK, N_HEADS, HEAD_DIM)


def _attn_sample_kernel(pt_ref, sel_ref, q_ref, kn_ref, vn_ref, slope_ref, ck_ref, cv_ref, o_ref,
                        kbuf, vbuf, sem, *, n_q, n_pages, past_len):
    b = pl.program_id(0)
    hh = pl.program_id(1)
    sel_base = (b * N_HEADS + hh) * n_q * MOBA_TOP_K

    def copies():
        out = []
        for qi in range(n_q):
            for r in range(MOBA_TOP_K):
                n = sel_ref[sel_base + qi * MOBA_TOP_K + r]
                for pg in range(PAGES_PER_BLOCK):
                    page = pt_ref[b * n_pages + n * PAGES_PER_BLOCK + pg]
                    dst = pl.ds(pg * PAGE_SIZE, PAGE_SIZE)
                    out.append(pltpu.make_async_copy(ck_ref.at[page, :, hh, :], kbuf.at[qi, r, dst, :], sem.at[0]))
                    out.append(pltpu.make_async_copy(cv_ref.at[page, :, hh, :], vbuf.at[qi, r, dst, :], sem.at[1]))
        return out

    cps = copies()
    for cp in cps:
        cp.start()
    for cp in cps:
        cp.wait()

    slope = slope_ref[0][:, 0:1]
    k_new = kn_ref[...]
    v_new = vn_ref[...]
    key_off = lax.broadcasted_iota(jnp.int32, (MOBA_BLOCK, 1), 0)
    new_off = lax.broadcasted_iota(jnp.int32, (SAMPLE_ROWS, 1), 0)
    out_row = lax.broadcasted_iota(jnp.int32, (SAMPLE_ROWS, HEAD_DIM), 0)
    out = jnp.zeros((SAMPLE_ROWS, HEAD_DIM), F32)
    for qi in range(n_q):
        qrow = q_ref[qi:qi + 1, :]
        s_own = jnp.sum(k_new * qrow, axis=-1, keepdims=True) - slope * (qi - new_off).astype(F32)
        s_own = jnp.where(new_off <= qi, s_own, NEG)
        scores = []
        m = jnp.max(s_own, axis=0, keepdims=True)
        for r in range(MOBA_TOP_K):
            n = sel_ref[sel_base + qi * MOBA_TOP_K + r]
            dist = (past_len + qi - n * MOBA_BLOCK - key_off).astype(F32)
            s = jnp.sum(kbuf[qi, r] * qrow, axis=-1, keepdims=True) - slope * dist
            scores.append(s)
            m = jnp.maximum(m, jnp.max(s, axis=0, keepdims=True))
        p_own = jnp.exp(s_own - m)
        l = jnp.sum(p_own, axis=0, keepdims=True)
        acc = jnp.sum(p_own * v_new, axis=0, keepdims=True)
        for r in range(MOBA_TOP_K):
            p = jnp.exp(scores[r] - m)
            l = l + jnp.sum(p, axis=0, keepdims=True)
            acc = acc + jnp.sum(p * vbuf[qi, r], axis=0, keepdims=True)
        out = jnp.where(out_row == qi, acc / l, out)
    o_ref[...] = out.astype(BF16)


def _attn_sample(q, k_new, v_new, sel, cache_k, cache_v, page_table, slopes, *, n_q):
    bsz, n_pages = page_table.shape
    kern = functools.partial(_attn_sample_kernel, n_q=n_q, n_pages=n_pages, past_len=n_pages * PAGE_SIZE)
    tok = pl.BlockSpec((SAMPLE_ROWS, HEAD_DIM), lambda b, h, pt_ref, sel_ref: (b, h))
    return pl.pallas_call(
        kern,
        out_shape=jax.ShapeDtypeStruct(q.shape, BF16),
        grid_spec=pltpu.PrefetchScalarGridSpec(
            num_scalar_prefetch=2,
            grid=(bsz, N_HEADS),
            in_specs=[
                tok, tok, tok,
                pl.BlockSpec((1, 1, LANES), lambda b, h, pt_ref, sel_ref: (h, 0, 0)),
                pl.BlockSpec(memory_space=pl.ANY),
                pl.BlockSpec(memory_space=pl.ANY),
            ],
            out_specs=tok,
            scratch_shapes=[
                pltpu.VMEM((n_q, MOBA_TOP_K, MOBA_BLOCK, HEAD_DIM), F32),
                pltpu.VMEM((n_q, MOBA_TOP_K, MOBA_BLOCK, HEAD_DIM), F32),
                pltpu.SemaphoreType.DMA((2,)),
            ],
        ),
        compiler_params=_cparams("arbitrary", "arbitrary"),
        name="attn_sample",
    )(page_table.reshape(-1), sel, q, k_new, v_new, slopes, cache_k, cache_v)


def _oproj_kernel(h_ref, o_ref, w_ref, out_ref):
    out_ref[...] = h_ref[...] + _dot(o_ref[...], w_ref[...])


def _oproj(h, o, j, w, *, tm):
    n = h.shape[0]
    tok = pl.BlockSpec((tm, D_MODEL), lambda i: (i, 0))
    return pl.pallas_call(
        _oproj_kernel,
        out_shape=jax.ShapeDtypeStruct(h.shape, F32),
        grid=(n // tm,),
        in_specs=[tok, tok, pl.BlockSpec((None, D_MODEL, D_MODEL), lambda i: (j, 0, 0))],
        out_specs=tok,
        compiler_params=_cparams("parallel"),
        name="oproj",
    )(h, o, w["w_o"])


def _prepare_weights(norm_mix, norm_ffn, norm_ple, w_pool, pool_scale, norm_kv, w_kv, k_norm, w_q, q_norm, w_o,
                     router_group_w, router_group_b, router_expert_w, router_expert_b, w_gate, w_up, w_down,
                     w_ple, w_ple_gate):
    pad = LANES - N_GROUPS - N_EXPERTS
    wr = jnp.concatenate([router_group_w, router_expert_w, jnp.zeros((DEPTH, D_MODEL, pad), F32)], axis=-1)
    br = jnp.concatenate([router_group_b, router_expert_b, jnp.zeros((DEPTH, pad), F32)], axis=-1)
    wr_hi, wr_lo = _split(wr)
    wp_hi, wp_lo = _split(w_pool)
    return dict(
        norm_mix=norm_mix.reshape(DEPTH, 1, D_MODEL), norm_ffn=norm_ffn.reshape(DEPTH, 1, D_MODEL),
        norm_ple=norm_ple.reshape(DEPTH, 1, D_MODEL), wp_hi=wp_hi, wp_lo=wp_lo,
        pool_scale=pool_scale.reshape(N_A_LAYERS, 1, D_MODEL), norm_kv=norm_kv.reshape(1, D_MODEL),
        w_kv=w_kv.astype(BF16), k_norm=k_norm.reshape(1, HEAD_DIM), w_q=w_q.astype(BF16),
        q_norm=q_norm.reshape(-1, 1, HEAD_DIM), w_o=w_o.astype(BF16),
        wr_hi=wr_hi, wr_lo=wr_lo, br=br.reshape(DEPTH, 1, LANES),
        w_gate=w_gate.astype(BF16), w_up=w_up.astype(BF16), w_down=w_down.astype(BF16),
        w_ple=w_ple.astype(BF16), w_ple_gate=w_ple_gate.astype(BF16),
    )


def kernel(x_prompt, x_sample, p_prompt, p_sample, state_pool, cache_k, cache_v, page_table, norm_mix, norm_ffn, norm_ple, w_pool, pool_scale, norm_kv, w_kv, k_norm, w_q, q_norm, w_o, router_group_w, router_group_b, router_expert_w, router_expert_b, w_gate, w_up, w_down, w_ple, w_ple_gate):
    w = _prepare_weights(norm_mix, norm_ffn, norm_ple, w_pool, pool_scale, norm_kv, w_kv, k_norm, w_q, q_norm, w_o,
                         router_group_w, router_group_b, router_expert_w, router_expert_b, w_gate, w_up, w_down,
                         w_ple, w_ple_gate)
    slopes = jnp.exp2(-ALIBI_MAX_BIAS * jnp.arange(1, N_HEADS + 1, dtype=F32) / N_HEADS)
    slopes = jnp.broadcast_to(slopes[:, None, None], (N_HEADS, 1, LANES))

    bsz, seq, _ = x_prompt.shape
    dec_b, dec_q, _ = x_sample.shape
    n_pages = page_table.shape[1]
    past_len = n_pages * PAGE_SIZE
    blocks_per_seq = seq // MOBA_BLOCK
    n_past_blk = past_len // MOBA_BLOCK
    n_p, n_s = bsz * seq, dec_b * SAMPLE_ROWS
    row_pad = ((0, 0), (0, 0), (0, SAMPLE_ROWS - dec_q), (0, 0))

    hp = x_prompt
    hs = jnp.pad(x_sample, row_pad[1:])
    pp = p_prompt.reshape(DEPTH, n_p, PLE_DIM)
    ps = jnp.pad(p_sample, row_pad).reshape(DEPTH, n_s, PLE_DIM)
    pre_p = jnp.zeros((bsz, POOL_HALO, D_MODEL), F32)
    pre_s = jnp.pad(state_pool, ((0, 0), (0, 0), (1, 0), (0, 0)))

    pool_p, pool_s = [], []
    kv_p = kv_s = km_p = km_s = None
    for l in range(DEPTH):
        if l < N_A_LAYERS:
            g, sc = w["norm_mix"][l], w["pool_scale"][l]
            hp, st = _pool_mixer(hp.reshape(bsz, seq, D_MODEL), pre_p, g, w["wp_hi"][l], w["wp_lo"][l], sc,
                                 ts=512, n_valid=512, pos0=0)
            pool_p.append(st[:, 1:])
            hs, st = _pool_mixer(hs.reshape(dec_b, SAMPLE_ROWS, D_MODEL), pre_s[l], g, w["wp_hi"][l], w["wp_lo"][l], sc,
                                 ts=SAMPLE_ROWS, n_valid=dec_q, pos0=past_len)
            pool_s.append(st[:, 1:])
            hp, hs = hp.reshape(n_p, D_MODEL), hs.reshape(n_s, D_MODEL)
        else:
            j = l - N_A_LAYERS
            q, bias = _q_prompt(hp, j, w, *km_p, blocks_per_seq=blocks_per_seq)
            o = _attn_prompt(q, kv_p[2], kv_p[3], bias, slopes, bsz=bsz, blocks_per_seq=blocks_per_seq)
            hp = _oproj(hp, o, j, w, tm=512)
            q, idx = _q_sample(hs, j, w, *km_s, n_blk=n_past_blk)
            sel = idx.reshape(MOBA_TOP_K, dec_b, SAMPLE_ROWS, LANES)[:, :, :dec_q, :N_HEADS]
            sel = sel.transpose(1, 3, 2, 0).reshape(-1)
            o = _attn_sample(q, kv_s[0], kv_s[1], sel, cache_k, cache_v, page_table, slopes, n_q=dec_q)
            hs = _oproj(hs, o, j, w, tm=n_s)
        hp = _moe(hp, l, w, tm=512)
        hs = _moe(hs, l, w, tm=n_s)
        hp = _ple(hp, pp, l, w, tm=512)
        hs = _ple(hs, ps, l, w, tm=n_s)
        if l == N_A_LAYERS - 1:
            kv_p = _shared_kv(hp, w, tm=MOBA_BLOCK)
            kv_s = _shared_kv(hs, w, tm=n_s)
            km_p = _block_diag_means(kv_p[4].reshape(bsz, blocks_per_seq, N_HEADS, HEAD_DIM), blocks_per_seq)
            km_s = _block_diag_means(_kmean_cache(cache_k, page_table), n_past_blk)

    heads = (N_HEADS, HEAD_DIM)
    y_prompt = hp.reshape(bsz, seq, D_MODEL)
    y_sample = hs.reshape(dec_b, SAMPLE_ROWS, D_MODEL)[:, :dec_q]
    k_prompt, v_prompt = (t.reshape(bsz, seq, *heads) for t in kv_p[:2])
    k_sample, v_sample = (t.reshape(dec_b, SAMPLE_ROWS, *heads)[:, :dec_q] for t in kv_s[:2])
    return (y_prompt, y_sample, jnp.stack(pool_p), jnp.stack(pool_s), k_prompt, v_prompt, k_sample, v_sample)
```

```python
import functools

import jax
import jax.numpy as jnp
from jax import lax
from jax.experimental import pallas as pl
from jax.experimental.pallas import tpu as pltpu

F32 = jnp.float32
BF16 = jnp.bfloat16

D_MODEL = 1024
DEPTH = 4
N_A_LAYERS = DEPTH // 2
POOL_WINDOWS = (2, 4, 8, 16)
POOL_GROUP_DIM = D_MODEL // len(POOL_WINDOWS)
POOL_STATE = max(POOL_WINDOWS) - 1
POOL_HALO = POOL_STATE + 1
HEAD_DIM = 128
N_HEADS = D_MODEL // HEAD_DIM
MOBA_BLOCK = 256
MOBA_TOP_K = 3
PAGE_SIZE = 128
PAGES_PER_BLOCK = MOBA_BLOCK // PAGE_SIZE
ALIBI_MAX_BIAS = 8.0
N_GROUPS = 4
EXPERTS_PER_GROUP = 4
N_EXPERTS = N_GROUPS * EXPERTS_PER_GROUP
D_EXPERT = D_MODEL // 2
PLE_DIM = 256
NORM_EPS = 1e-6
LANES = 128
SAMPLE_ROWS = 8
NEG = -1e30
VMEM_LIMIT = 48 * 1024 * 1024


def _cparams(*sem):
    return pltpu.CompilerParams(dimension_semantics=sem, vmem_limit_bytes=VMEM_LIMIT)


def _rms(x, g):
    return x * lax.rsqrt(jnp.mean(x * x, axis=-1, keepdims=True) + NORM_EPS) * g


def _dot(a, b):
    return jnp.dot(a, b, preferred_element_type=F32)


def _split(x):
    bits = lax.bitcast_convert_type(x, jnp.uint32) & jnp.uint32(0xFFFF0000)
    hi = lax.bitcast_convert_type(bits, F32)
    return hi.astype(BF16), (x - hi).astype(BF16)


def _dot3(a, b_hi, b_lo):
    a_hi, a_lo = _split(a)
    return _dot(a_hi, b_hi) + (_dot(a_hi, b_lo) + _dot(a_lo, b_hi))


def _sigmoid(x):
    return 1.0 / (1.0 + jnp.exp(-x))


def _pool_kernel(h_ref, pre_ref, g_ref, whi_ref, wlo_ref, sc_ref, o_ref, st_ref, ext_ref, *, ts, n_valid, pos0):
    s = pl.program_id(1)

    @pl.when(s == 0)
    def _():
        ext_ref[0:POOL_HALO, :] = pre_ref[0]

    x = h_ref[0]
    hn = _rms(x, g_ref[...])
    ext_ref[POOL_HALO:POOL_HALO + ts, :] = hn
    pos = pos0 + s * ts + lax.broadcasted_iota(jnp.int32, (ts, 1), 0)
    for g, w in enumerate(POOL_WINDOWS):
        c0, c1 = g * POOL_GROUP_DIM, (g + 1) * POOL_GROUP_DIM
        acc = hn[:, c0:c1]
        for j in range(1, w):
            acc = acc + ext_ref[POOL_HALO - j:POOL_HALO - j + ts, c0:c1]
        cnt = jnp.minimum(pos + 1, w).astype(F32)
        diff = acc / cnt - hn[:, c0:c1]
        y = _dot3(diff, whi_ref[g], wlo_ref[g])
        o_ref[0, :, c0:c1] = x[:, c0:c1] + y * sc_ref[:, c0:c1]
    tail = ext_ref[n_valid:n_valid + POOL_HALO, :]
    st_ref[0] = tail
    ext_ref[0:POOL_HALO, :] = tail


def _pool_mixer(h, prefix, g, w_hi, w_lo, scale, *, ts, n_valid, pos0):
    bsz, seq, _ = h.shape
    kern = functools.partial(_pool_kernel, ts=ts, n_valid=n_valid, pos0=pos0)
    return pl.pallas_call(
        kern,
        out_shape=(jax.ShapeDtypeStruct(h.shape, F32), jax.ShapeDtypeStruct((bsz, POOL_HALO, D_MODEL), F32)),
        grid=(bsz, seq // ts),
        in_specs=[
            pl.BlockSpec((1, ts, D_MODEL), lambda b, s: (b, s, 0)),
            pl.BlockSpec((1, POOL_HALO, D_MODEL), lambda b, s: (b, 0, 0)),
            pl.BlockSpec((1, D_MODEL), lambda b, s: (0, 0)),
            pl.BlockSpec((len(POOL_WINDOWS), POOL_GROUP_DIM, POOL_GROUP_DIM), lambda b, s: (0, 0, 0)),
            pl.BlockSpec((len(POOL_WINDOWS), POOL_GROUP_DIM, POOL_GROUP_DIM), lambda b, s: (0, 0, 0)),
            pl.BlockSpec((1, D_MODEL), lambda b, s: (0, 0)),
        ],
        out_specs=(
            pl.BlockSpec((1, ts, D_MODEL), lambda b, s: (b, s, 0)),
            pl.BlockSpec((1, POOL_HALO, D_MODEL), lambda b, s: (b, 0, 0)),
        ),
        scratch_shapes=[pltpu.VMEM((POOL_HALO + ts, D_MODEL), F32)],
        compiler_params=_cparams("parallel", "arbitrary"),
        name="pool_mixer",
    )(h, prefix, g, w_hi, w_lo, scale)


def _route(xn, wr_hi, wr_lo, br):
    logits = _dot3(xn, wr_hi, wr_lo) + br
    lane = lax.broadcasted_iota(jnp.int32, logits.shape, 1)
    big = jnp.int32(1 << 20)
    lg = jnp.where(lane < N_GROUPS, logits, -jnp.inf)
    mg = jnp.max(lg, axis=-1, keepdims=True)
    g_idx = jnp.min(jnp.where(lg == mg, lane, big), axis=-1, keepdims=True)
    p_grp = 1.0 / jnp.sum(jnp.exp(lg - mg), axis=-1, keepdims=True)
    e0 = N_GROUPS + EXPERTS_PER_GROUP * g_idx
    le = jnp.where((lane >= e0) & (lane < e0 + EXPERTS_PER_GROUP), logits, -jnp.inf)
    m1 = jnp.max(le, axis=-1, keepdims=True)
    i1 = jnp.min(jnp.where(le == m1, lane, big), axis=-1, keepdims=True)
    le2 = jnp.where(lane == i1, -jnp.inf, le)
    m2 = jnp.max(le2, axis=-1, keepdims=True)
    i2 = jnp.min(jnp.where(le2 == m2, lane, big), axis=-1, keepdims=True)
    e2 = jnp.exp(m2 - m1)
    w1 = p_grp / (1.0 + e2)
    w2 = p_grp * e2 / (1.0 + e2)
    return jnp.where(lane == i1, w1, 0.0) + jnp.where(lane == i2, w2, 0.0)


def _moe_kernel(h_ref, g_ref, wrh_ref, wrl_ref, br_ref, wg_ref, wu_ref, wd_ref, o_ref, xn_ref, gates_ref, acc_ref):
    e = pl.program_id(1)

    @pl.when(e == 0)
    def _():
        xn = _rms(h_ref[...], g_ref[...])
        xn_ref[...] = xn.astype(BF16)
        gates_ref[...] = _route(xn, wrh_ref[...], wrl_ref[...], br_ref[...])
        acc_ref[...] = jnp.zeros_like(acc_ref)

    xb = xn_ref[...]
    a = _dot(xb, wg_ref[...])
    u = _dot(xb, wu_ref[...])
    gates = gates_ref[...]
    lane = lax.broadcasted_iota(jnp.int32, gates.shape, 1)
    gate = jnp.sum(jnp.where(lane == e + N_GROUPS, gates, 0.0), axis=-1, keepdims=True)
    hid = a * _sigmoid(a) * u * gate
    acc_ref[...] += _dot(hid.astype(BF16), wd_ref[...])

    @pl.when(e == N_EXPERTS - 1)
    def _():
        o_ref[...] = h_ref[...] + acc_ref[...]


def _moe(h, l, w, *, tm):
    n = h.shape[0]
    return pl.pallas_call(
        _moe_kernel,
        out_shape=jax.ShapeDtypeStruct(h.shape, F32),
        grid=(n // tm, N_EXPERTS),
        in_specs=[
            pl.BlockSpec((tm, D_MODEL), lambda i, e: (i, 0)),
            pl.BlockSpec((None, 1, D_MODEL), lambda i, e: (l, 0, 0)),
            pl.BlockSpec((None, D_MODEL, LANES), lambda i, e: (l, 0, 0)),
            pl.BlockSpec((None, D_MODEL, LANES), lambda i, e: (l, 0, 0)),
            pl.BlockSpec((None, 1, LANES), lambda i, e: (l, 0, 0)),
            pl.BlockSpec((None, None, D_MODEL, D_EXPERT), lambda i, e: (l, e, 0, 0)),
            pl.BlockSpec((None, None, D_MODEL, D_EXPERT), lambda i, e: (l, e, 0, 0)),
            pl.BlockSpec((None, None, D_EXPERT, D_MODEL), lambda i, e: (l, e, 0, 0)),
        ],
        out_specs=pl.BlockSpec((tm, D_MODEL), lambda i, e: (i, 0)),
        scratch_shapes=[
            pltpu.VMEM((tm, D_MODEL), BF16),
            pltpu.VMEM((tm, LANES), F32),
            pltpu.VMEM((tm, D_MODEL), F32),
        ],
        compiler_params=_cparams("parallel", "arbitrary"),
        name="hier_moe",
    )(h, w["norm_ffn"], w["wr_hi"], w["wr_lo"], w["br"], w["w_gate"], w["w_up"], w["w_down"])


def _ple_kernel(h_ref, p_ref, g_ref, wg_ref, wp_ref, o_ref):
    h = h_ref[...]
    hn = _rms(h, g_ref[...]).astype(BF16)
    gate = _sigmoid(_dot(hn, wg_ref[...]))
    o_ref[...] = h + _dot(p_ref[...].astype(BF16), wp_ref[...]) * gate


def _ple(h, p, l, w, *, tm):
    n = h.shape[0]
    return pl.pallas_call(
        _ple_kernel,
        out_shape=jax.ShapeDtypeStruct(h.shape, F32),
        grid=(n // tm,),
        in_specs=[
            pl.BlockSpec((tm, D_MODEL), lambda i: (i, 0)),
            pl.BlockSpec((None, tm, PLE_DIM), lambda i: (l, i, 0)),
            pl.BlockSpec((None, 1, D_MODEL), lambda i: (l, 0, 0)),
            pl.BlockSpec((None, D_MODEL, D_MODEL), lambda i: (l, 0, 0)),
            pl.BlockSpec((None, PLE_DIM, D_MODEL), lambda i: (l, 0, 0)),
        ],
        out_specs=pl.BlockSpec((tm, D_MODEL), lambda i: (i, 0)),
        compiler_params=_cparams("parallel"),
        name="ple",
    )(h, p, w["norm_ple"], w["w_ple_gate"], w["w_ple"])


def _head_rms(x, g):
    outs = []
    for hh in range(N_HEADS):
        outs.append(_rms(x[:, hh * HEAD_DIM:(hh + 1) * HEAD_DIM], g))
    return jnp.concatenate(outs, axis=-1)


FEAT_BLOCK_START = 16
FEAT_OFFSET = 17
FEAT_ONES_ROWS = 8


def _alibi_slope(hh):
    return 2.0 ** (-ALIBI_MAX_BIAS * (hh + 1) / N_HEADS)


assert all(_alibi_slope(hh) == 2.0 ** round(-ALIBI_MAX_BIAS * (hh + 1) / N_HEADS) for hh in range(N_HEADS))


def _kv_kernel(h_ref, g_ref, w_ref, kn_ref, k_ref, v_ref, *attn_refs, blocks_per_seq):
    hn = _rms(h_ref[...], g_ref[...]).astype(BF16)
    kv = _dot(hn, w_ref[...])
    k = _head_rms(kv[:, :D_MODEL], kn_ref[...])
    v = kv[:, D_MODEL:]
    k_ref[...] = k
    v_ref[...] = v
    if not attn_refs:
        return
    ka_ref, vt_ref, km_ref = attn_refs
    km_ref[0] = jnp.sum(k, axis=0, keepdims=True) / MOBA_BLOCK
    tm = k.shape[0]
    blk = pl.program_id(0) % blocks_per_seq
    lane = lax.broadcasted_iota(jnp.int32, (tm, HEAD_DIM), 1)
    off = lax.broadcasted_iota(jnp.int32, (tm, HEAD_DIM), 0).astype(F32)
    onehot = jnp.where(lane == blk, 1.0, 0.0)
    start = (blk * MOBA_BLOCK).astype(F32)
    for hh in range(N_HEADS):
        slope = _alibi_slope(hh)
        feat = onehot + jnp.where(lane == FEAT_BLOCK_START, slope * start, 0.0) \
            + jnp.where(lane == FEAT_OFFSET, slope * off, 0.0)
        c0 = hh * 2 * HEAD_DIM
        ka_ref[:, c0:c0 + HEAD_DIM] = k[:, hh * HEAD_DIM:(hh + 1) * HEAD_DIM].astype(BF16)
        ka_ref[:, c0 + HEAD_DIM:c0 + 2 * HEAD_DIM] = feat.astype(BF16)
        vt_ref[hh, 0] = v[:, hh * HEAD_DIM:(hh + 1) * HEAD_DIM].T.astype(BF16)


def _shared_kv(h, w, *, tm, blocks_per_seq=None):
    n = h.shape[0]
    tok = pl.BlockSpec((tm, D_MODEL), lambda i: (i, 0))
    out_shape = [jax.ShapeDtypeStruct((n, D_MODEL), F32), jax.ShapeDtypeStruct((n, D_MODEL), F32)]
    out_specs = [tok, tok]
    if blocks_per_seq is not None:
        out_shape += [
            jax.ShapeDtypeStruct((n, 2 * D_MODEL), BF16),
            jax.ShapeDtypeStruct((N_HEADS, n // tm, HEAD_DIM, tm), BF16),
            jax.ShapeDtypeStruct((n // tm, 1, D_MODEL), F32),
        ]
        out_specs += [
            pl.BlockSpec((tm, 2 * D_MODEL), lambda i: (i, 0)),
            pl.BlockSpec((N_HEADS, 1, HEAD_DIM, tm), lambda i: (0, i, 0, 0)),
            pl.BlockSpec((1, 1, D_MODEL), lambda i: (i, 0, 0)),
        ]
    return pl.pallas_call(
        functools.partial(_kv_kernel, blocks_per_seq=blocks_per_seq),
        out_shape=tuple(out_shape),
        grid=(n // tm,),
        in_specs=[
            tok,
            pl.BlockSpec((1, D_MODEL), lambda i: (0, 0)),
            pl.BlockSpec((D_MODEL, 2 * D_MODEL), lambda i: (0, 0)),
            pl.BlockSpec((1, HEAD_DIM), lambda i: (0, 0)),
        ],
        out_specs=tuple(out_specs),
        compiler_params=_cparams("parallel"),
        name="shared_kv",
    )(h, w["norm_kv"], w["w_kv"], w["k_norm"])


def _rank_in_segments(gv, seg):
    n_lanes = gv.shape[-1]
    lane = lax.broadcasted_iota(jnp.int32, gv.shape, 1)
    n = lane & (seg - 1)
    rank = jnp.zeros(gv.shape, jnp.int32)
    for k in range(1, seg):
        ahead = pltpu.roll(gv, n_lanes - k, axis=1)
        behind = pltpu.roll(gv, seg - k, axis=1)
        wrap = n + k >= seg
        partner = jnp.where(wrap, behind, ahead)
        rank = rank + jnp.where(wrap, (partner >= gv).astype(jnp.int32), (partner > gv).astype(jnp.int32))
    return rank


def _queries(h_ref, g_ref, wq_ref, qn_ref):
    hn = _rms(h_ref[...], g_ref[...]).astype(BF16)
    return _head_rms(_dot(hn, wq_ref[...]), qn_ref[...])


def _q_prompt_kernel(h_ref, g_ref, wq_ref, qn_ref, kmh_ref, kml_ref, qt_ref, *, blocks_per_seq):
    own = pl.program_id(0) % blocks_per_seq
    q = _queries(h_ref, g_ref, wq_ref, qn_ref)
    tm = q.shape[0]
    gate = _dot3(q, kmh_ref[0], kml_ref[0])
    lane = lax.broadcasted_iota(jnp.int32, gate.shape, 1)
    blk = lane & (blocks_per_seq - 1)
    valid = blk < own
    gv = jnp.where(valid, gate, -jnp.inf)
    rank = _rank_in_segments(gv, blocks_per_seq)
    keep = jnp.where(valid, (rank < MOBA_TOP_K).astype(jnp.int32), (blk == own).astype(jnp.int32))
    bias_t = jnp.where(keep > 0, 0.0, NEG).T
    ones = jnp.ones((FEAT_ONES_ROWS, tm), F32)
    zeros = jnp.zeros((HEAD_DIM - blocks_per_seq - FEAT_ONES_ROWS, tm), F32)
    for hh in range(N_HEADS):
        qh = q[:, hh * HEAD_DIM:(hh + 1) * HEAD_DIM] * HEAD_DIM ** -0.5
        feat = jnp.concatenate([bias_t[hh * blocks_per_seq:(hh + 1) * blocks_per_seq], ones, zeros], axis=0)
        qt_ref[hh, 0, 0:HEAD_DIM, :] = qh.T.astype(BF16)
        qt_ref[hh, 0, HEAD_DIM:2 * HEAD_DIM, :] = feat.astype(BF16)


def _q_prompt(h, j, w, km_hi, km_lo, *, blocks_per_seq):
    assert blocks_per_seq == FEAT_BLOCK_START and N_HEADS * blocks_per_seq == LANES
    n = h.shape[0]
    tm = MOBA_BLOCK
    kern = functools.partial(_q_prompt_kernel, blocks_per_seq=blocks_per_seq)
    return pl.pallas_call(
        kern,
        out_shape=jax.ShapeDtypeStruct((N_HEADS, n // tm, 2 * HEAD_DIM, tm), BF16),
        grid=(n // tm,),
        in_specs=[
            pl.BlockSpec((tm, D_MODEL), lambda i: (i, 0)),
            pl.BlockSpec((None, 1, D_MODEL), lambda i: (N_A_LAYERS + j, 0, 0)),
            pl.BlockSpec((None, D_MODEL, D_MODEL), lambda i: (j, 0, 0)),
            pl.BlockSpec((None, 1, HEAD_DIM), lambda i: (j, 0, 0)),
            pl.BlockSpec((1, D_MODEL, LANES), lambda i: (i // blocks_per_seq, 0, 0)),
            pl.BlockSpec((1, D_MODEL, LANES), lambda i: (i // blocks_per_seq, 0, 0)),
        ],
        out_specs=pl.BlockSpec((N_HEADS, 1, 2 * HEAD_DIM, tm), lambda i: (0, i, 0, 0)),
        compiler_params=_cparams("parallel"),
        name="q_prompt",
    )(h, w["norm_mix"], w["w_q"], w["q_norm"], km_hi, km_lo)


def _q_sample_kernel(h_ref, g_ref, wq_ref, qn_ref, kmh_ref, kml_ref, q_ref, idx_ref, *, n_blk):
    q = _queries(h_ref, g_ref, wq_ref, qn_ref)
    q_ref[...] = q * HEAD_DIM ** -0.5
    gate = _dot3(q, kmh_ref[0], kml_ref[0])
    rank = _rank_in_segments(gate, n_blk)
    lane = lax.broadcasted_iota(jnp.int32, gate.shape, 1)
    blk = (lane & (n_blk - 1)).astype(F32)
    row = lax.broadcasted_iota(jnp.int32, (gate.shape[1], LANES), 0)
    col = lax.broadcasted_iota(jnp.int32, (gate.shape[1], LANES), 1)
    seg_of = jnp.where((row >= col * n_blk) & (row < (col + 1) * n_blk), 1.0, 0.0).astype(BF16)
    for r in range(MOBA_TOP_K):
        picked = jnp.where(rank == r, blk, 0.0).astype(BF16)
        idx_ref[r] = _dot(picked, seg_of).astype(jnp.int32)


def _q_sample(h, j, w, km_hi, km_lo, *, n_blk):
    n = h.shape[0]
    tm = SAMPLE_ROWS
    kern = functools.partial(_q_sample_kernel, n_blk=n_blk)
    return pl.pallas_call(
        kern,
        out_shape=(jax.ShapeDtypeStruct((n, D_MODEL), F32), jax.ShapeDtypeStruct((MOBA_TOP_K, n, LANES), jnp.int32)),
        grid=(n // tm,),
        in_specs=[
            pl.BlockSpec((tm, D_MODEL), lambda i: (i, 0)),
            pl.BlockSpec((None, 1, D_MODEL), lambda i: (N_A_LAYERS + j, 0, 0)),
            pl.BlockSpec((None, D_MODEL, D_MODEL), lambda i: (j, 0, 0)),
            pl.BlockSpec((None, 1, HEAD_DIM), lambda i: (j, 0, 0)),
            pl.BlockSpec((1, D_MODEL, N_HEADS * n_blk), lambda i: (i, 0, 0)),
            pl.BlockSpec((1, D_MODEL, N_HEADS * n_blk), lambda i: (i, 0, 0)),
        ],
        out_specs=(pl.BlockSpec((tm, D_MODEL), lambda i: (i, 0)), pl.BlockSpec((MOBA_TOP_K, tm, LANES), lambda i: (0, i, 0))),
        compiler_params=_cparams("parallel"),
        name="q_sample",
    )(h, w["norm_mix"], w["w_q"], w["q_norm"], km_hi, km_lo)


def _block_diag_means(km, n_blk):
    bsz = km.shape[0]
    eye = jnp.eye(N_HEADS, dtype=F32)
    t = km.transpose(0, 2, 3, 1)[:, :, :, None, :] * eye[None, :, None, :, None]
    return _split(t.reshape(bsz, D_MODEL, N_HEADS * n_blk))


HEADS_PER_STEP = 2


def _attn_prompt_kernel(qt_ref, ka_ref, vt_ref, o_ref):
    i = pl.program_id(2)
    blk = MOBA_BLOCK
    rows = lax.broadcasted_iota(jnp.int32, (blk, blk), 0)
    cols = lax.broadcasted_iota(jnp.int32, (blk, blk), 1)
    qts = [qt_ref[hp, 0] for hp in range(HEADS_PER_STEP)]

    def scores(hp, n):
        keys = ka_ref[pl.ds(pl.multiple_of(n * blk, blk), blk), hp * 2 * HEAD_DIM:(hp + 1) * 2 * HEAD_DIM]
        return _dot(keys, qts[hp])

    def absorb(s, vt, m, l, acc):
        m_new = jnp.maximum(m, jnp.max(s, axis=0, keepdims=True))
        alpha = jnp.exp(m - m_new)
        p = jnp.exp(s - m_new)
        l = alpha * l + jnp.sum(p, axis=0, keepdims=True)
        acc = alpha * acc + _dot(vt, p.astype(BF16))
        return m_new, l, acc

    init = tuple(
        (jnp.where(cols >= rows, scores(hp, i), NEG), jnp.full((1, blk), NEG, F32), jnp.zeros((1, blk), F32),
         jnp.zeros((HEAD_DIM, blk), F32))
        for hp in range(HEADS_PER_STEP))

    def body(n, carry):
        prev = jnp.where(n == 0, i, n - 1)
        out = []
        for hp in range(HEADS_PER_STEP):
            s, m, l, acc = carry[hp]
            s_next = scores(hp, n)
            out.append((s_next,) + absorb(s, vt_ref[hp, prev], m, l, acc))
        return tuple(out)

    carry = lax.fori_loop(0, i, body, init)
    last = jnp.where(i == 0, i, i - 1)
    for hp in range(HEADS_PER_STEP):
        s, m, l, acc = carry[hp]
        m, l, acc = absorb(s, vt_ref[hp, last], m, l, acc)
        o_ref[:, hp * HEAD_DIM:(hp + 1) * HEAD_DIM] = (acc / l).T.astype(BF16)


def _attn_prompt(qt, ka, vt, *, bsz, blocks_per_seq):
    n = ka.shape[0]
    seq = blocks_per_seq * MOBA_BLOCK
    hps = HEADS_PER_STEP
    return pl.pallas_call(
        _attn_prompt_kernel,
        out_shape=jax.ShapeDtypeStruct((n, D_MODEL), BF16),
        grid=(bsz, N_HEADS // hps, blocks_per_seq),
        in_specs=[
            pl.BlockSpec((hps, 1, 2 * HEAD_DIM, MOBA_BLOCK), lambda b, h, i: (h, b * blocks_per_seq + i, 0, 0)),
            pl.BlockSpec((seq, hps * 2 * HEAD_DIM), lambda b, h, i: (b, h)),
            pl.BlockSpec((hps, blocks_per_seq, HEAD_DIM, MOBA_BLOCK), lambda b, h, i: (h, b, 0, 0)),
        ],
        out_specs=pl.BlockSpec((MOBA_BLOCK, hps * HEAD_DIM), lambda b, h, i: (b * blocks_per_seq + i, h)),
        compiler_params=_cparams("parallel", "parallel", "arbitrary"),
        name="attn_prompt",
    )(qt, ka, vt)


def _kmean_cache_kernel(pt_ref, *refs):
    page_refs, o_ref = refs[:-1], refs[-1]
    for r in range(len(page_refs) // PAGES_PER_BLOCK):
        tot = jnp.sum(page_refs[2 * r][0], axis=0) + jnp.sum(page_refs[2 * r + 1][0], axis=0)
        o_ref[0, r] = tot / MOBA_BLOCK


def _kmean_cache(cache_k, page_table, *, pages_per_step=8):
    bsz, n_pages = page_table.shape
    steps = n_pages // pages_per_step
    blk_per_step = pages_per_step // PAGES_PER_BLOCK
    pt = page_table.reshape(-1)

    def page_spec(r):
        return pl.BlockSpec((1, PAGE_SIZE, N_HEADS, HEAD_DIM),
                            lambda b, j, pt_ref: (pt_ref[b * n_pages + j * pages_per_step + r], 0, 0, 0))

    out = pl.pallas_call(
        _kmean_cache_kernel,
        out_shape=jax.ShapeDtypeStruct((bsz * steps, blk_per_step, N_HEADS, HEAD_DIM), F32),
        grid_spec=pltpu.PrefetchScalarGridSpec(
            num_scalar_prefetch=1,
            grid=(bsz, steps),
            in_specs=[page_spec(r) for r in range(pages_per_step)],
            out_specs=pl.BlockSpec((1, blk_per_step, N_HEADS, HEAD_DIM), lambda b, j, pt_ref: (b * steps + j, 0, 0, 0)),
        ),
        compiler_params=_cparams("parallel", "arbitrary"),
        name="kmean_cache",
    )(pt, *([cache_k] * pages_per_step))
    return out.reshape(bsz, n_pages // PAGES_PER_BLOCK, N_HEADS, HEAD_DIM)


def _attn_sample_kernel(pt_ref, sel_ref, q_ref, kn_ref, vn_ref, slope_ref, ck_ref, cv_ref, o_ref,
                        kbuf, vbuf, sem, *, n_q, n_pages, past_len):
    b = pl.program_id(0)
    hh = pl.program_id(1)
    sel_base = (b * N_HEADS + hh) * n_q * MOBA_TOP_K

    def copies():
        out = []
        for qi in range(n_q):
            for r in range(MOBA_TOP_K):
                n = sel_ref[sel_base + qi * MOBA_TOP_K + r]
                for pg in range(PAGES_PER_BLOCK):
                    page = pt_ref[b * n_pages + n * PAGES_PER_BLOCK + pg]
                    dst = pl.ds(pg * PAGE_SIZE, PAGE_SIZE)
                    out.append(pltpu.make_async_copy(ck_ref.at[page, :, hh, :], kbuf.at[qi, r, dst, :], sem.at[0]))
                    out.append(pltpu.make_async_copy(cv_ref.at[page, :, hh, :], vbuf.at[qi, r, dst, :], sem.at[1]))
        return out

    cps = copies()
    for cp in cps:
        cp.start()
    for cp in cps:
        cp.wait()

    slope = slope_ref[0][:, 0:1]
    k_new = kn_ref[...]
    v_new = vn_ref[...]
    key_off = lax.broadcasted_iota(jnp.int32, (MOBA_BLOCK, 1), 0)
    new_off = lax.broadcasted_iota(jnp.int32, (SAMPLE_ROWS, 1), 0)
    out_row = lax.broadcasted_iota(jnp.int32, (SAMPLE_ROWS, HEAD_DIM), 0)
    out = jnp.zeros((SAMPLE_ROWS, HEAD_DIM), F32)
    for qi in range(n_q):
        qrow = q_ref[qi:qi + 1, :]
        s_own = jnp.sum(k_new * qrow, axis=-1, keepdims=True) - slope * (qi - new_off).astype(F32)
        s_own = jnp.where(new_off <= qi, s_own, NEG)
        scores = []
        m = jnp.max(s_own, axis=0, keepdims=True)
        for r in range(MOBA_TOP_K):
            n = sel_ref[sel_base + qi * MOBA_TOP_K + r]
            dist = (past_len + qi - n * MOBA_BLOCK - key_off).astype(F32)
            s = jnp.sum(kbuf[qi, r] * qrow, axis=-1, keepdims=True) - slope * dist
            scores.append(s)
            m = jnp.maximum(m, jnp.max(s, axis=0, keepdims=True))
        p_own = jnp.exp(s_own - m)
        l = jnp.sum(p_own, axis=0, keepdims=True)
        acc = jnp.sum(p_own * v_new, axis=0, keepdims=True)
        for r in range(MOBA_TOP_K):
            p = jnp.exp(scores[r] - m)
            l = l + jnp.sum(p, axis=0, keepdims=True)
            acc = acc + jnp.sum(p * vbuf[qi, r], axis=0, keepdims=True)
        out = jnp.where(out_row == qi, acc / l, out)
    o_ref[...] = out.astype(BF16)


def _attn_sample(q, k_new, v_new, sel, cache_k, cache_v, page_table, slopes, *, n_q):
    bsz, n_pages = page_table.shape
    kern = functools.partial(_attn_sample_kernel, n_q=n_q, n_pages=n_pages, past_len=n_pages * PAGE_SIZE)
    tok = pl.BlockSpec((SAMPLE_ROWS, HEAD_DIM), lambda b, h, pt_ref, sel_ref: (b, h))
    return pl.pallas_call(
        kern,
        out_shape=jax.ShapeDtypeStruct(q.shape, BF16),
        grid_spec=pltpu.PrefetchScalarGridSpec(
            num_scalar_prefetch=2,
            grid=(bsz, N_HEADS),
            in_specs=[
                tok, tok, tok,
                pl.BlockSpec((1, 1, LANES), lambda b, h, pt_ref, sel_ref: (h, 0, 0)),
                pl.BlockSpec(memory_space=pl.ANY),
                pl.BlockSpec(memory_space=pl.ANY),
            ],
            out_specs=tok,
            scratch_shapes=[
                pltpu.VMEM((n_q, MOBA_TOP_K, MOBA_BLOCK, HEAD_DIM), F32),
                pltpu.VMEM((n_q, MOBA_TOP_K, MOBA_BLOCK, HEAD_DIM), F32),
                pltpu.SemaphoreType.DMA((2,)),
            ],
        ),
        compiler_params=_cparams("arbitrary", "arbitrary"),
        name="attn_sample",
    )(page_table.reshape(-1), sel, q, k_new, v_new, slopes, cache_k, cache_v)


def _oproj_kernel(h_ref, o_ref, w_ref, out_ref):
    out_ref[...] = h_ref[...] + _dot(o_ref[...], w_ref[...])


def _oproj(h, o, j, w, *, tm):
    n = h.shape[0]
    tok = pl.BlockSpec((tm, D_MODEL), lambda i: (i, 0))
    return pl.pallas_call(
        _oproj_kernel,
        out_shape=jax.ShapeDtypeStruct(h.shape, F32),
        grid=(n // tm,),
        in_specs=[tok, tok, pl.BlockSpec((None, D_MODEL, D_MODEL), lambda i: (j, 0, 0))],
        out_specs=tok,
        compiler_params=_cparams("parallel"),
        name="oproj",
    )(h, o, w["w_o"])


def _prepare_weights(norm_mix, norm_ffn, norm_ple, w_pool, pool_scale, norm_kv, w_kv, k_norm, w_q, q_norm, w_o,
                     router_group_w, router_group_b, router_expert_w, router_expert_b, w_gate, w_up, w_down,
                     w_ple, w_ple_gate):
    pad = LANES - N_GROUPS - N_EXPERTS
    wr = jnp.concatenate([router_group_w, router_expert_w, jnp.zeros((DEPTH, D_MODEL, pad), F32)], axis=-1)
    br = jnp.concatenate([router_group_b, router_expert_b, jnp.zeros((DEPTH, pad), F32)], axis=-1)
    wr_hi, wr_lo = _split(wr)
    wp_hi, wp_lo = _split(w_pool)
    return dict(
        norm_mix=norm_mix.reshape(DEPTH, 1, D_MODEL), norm_ffn=norm_ffn.reshape(DEPTH, 1, D_MODEL),
        norm_ple=norm_ple.reshape(DEPTH, 1, D_MODEL), wp_hi=wp_hi, wp_lo=wp_lo,
        pool_scale=pool_scale.reshape(N_A_LAYERS, 1, D_MODEL), norm_kv=norm_kv.reshape(1, D_MODEL),
        w_kv=w_kv.astype(BF16), k_norm=k_norm.reshape(1, HEAD_DIM), w_q=w_q.astype(BF16),
        q_norm=q_norm.reshape(-1, 1, HEAD_DIM), w_o=w_o.astype(BF16),
        wr_hi=wr_hi, wr_lo=wr_lo, br=br.reshape(DEPTH, 1, LANES),
        w_gate=w_gate.astype(BF16), w_up=w_up.astype(BF16), w_down=w_down.astype(BF16),
        w_ple=w_ple.astype(BF16), w_ple_gate=w_ple_gate.astype(BF16),
    )


def kernel(x_prompt, x_sample, p_prompt, p_sample, state_pool, cache_k, cache_v, page_table, norm_mix, norm_ffn, norm_ple, w_pool, pool_scale, norm_kv, w_kv, k_norm, w_q, q_norm, w_o, router_group_w, router_group_b, router_expert_w, router_expert_b, w_gate, w_up, w_down, w_ple, w_ple_gate):
    w = _prepare_weights(norm_mix, norm_ffn, norm_ple, w_pool, pool_scale, norm_kv, w_kv, k_norm, w_q, q_norm, w_o,
                         router_group_w, router_group_b, router_expert_w, router_expert_b, w_gate, w_up, w_down,
                         w_ple, w_ple_gate)
    slopes = jnp.exp2(-ALIBI_MAX_BIAS * jnp.arange(1, N_HEADS + 1, dtype=F32) / N_HEADS)
    slopes = jnp.broadcast_to(slopes[:, None, None], (N_HEADS, 1, LANES))

    bsz, seq, _ = x_prompt.shape
    dec_b, dec_q, _ = x_sample.shape
    n_pages = page_table.shape[1]
    past_len = n_pages * PAGE_SIZE
    blocks_per_seq = seq // MOBA_BLOCK
    n_past_blk = past_len // MOBA_BLOCK
    n_p, n_s = bsz * seq, dec_b * SAMPLE_ROWS
    row_pad = ((0, 0), (0, 0), (0, SAMPLE_ROWS - dec_q), (0, 0))

    hp = x_prompt
    hs = jnp.pad(x_sample, row_pad[1:])
    pp = p_prompt.reshape(DEPTH, n_p, PLE_DIM)
    ps = jnp.pad(p_sample, row_pad).reshape(DEPTH, n_s, PLE_DIM)
    pre_p = jnp.zeros((bsz, POOL_HALO, D_MODEL), F32)
    pre_s = jnp.pad(state_pool, ((0, 0), (0, 0), (1, 0), (0, 0)))

    pool_p, pool_s = [], []
    kv_p = kv_s = km_p = km_s = None
    for l in range(DEPTH):
        if l < N_A_LAYERS:
            g, sc = w["norm_mix"][l], w["pool_scale"][l]
            hp, st = _pool_mixer(hp.reshape(bsz, seq, D_MODEL), pre_p, g, w["wp_hi"][l], w["wp_lo"][l], sc,
                                 ts=512, n_valid=512, pos0=0)
            pool_p.append(st[:, 1:])
            hs, st = _pool_mixer(hs.reshape(dec_b, SAMPLE_ROWS, D_MODEL), pre_s[l], g, w["wp_hi"][l], w["wp_lo"][l], sc,
                                 ts=SAMPLE_ROWS, n_valid=dec_q, pos0=past_len)
            pool_s.append(st[:, 1:])
            hp, hs = hp.reshape(n_p, D_MODEL), hs.reshape(n_s, D_MODEL)
        else:
            j = l - N_A_LAYERS
            qt = _q_prompt(hp, j, w, *km_p, blocks_per_seq=blocks_per_seq)
            o = _attn_prompt(qt, kv_p[2], kv_p[3], bsz=bsz, blocks_per_seq=blocks_per_seq)
            hp = _oproj(hp, o, j, w, tm=512)
            q, idx = _q_sample(hs, j, w, *km_s, n_blk=n_past_blk)
            sel = idx.reshape(MOBA_TOP_K, dec_b, SAMPLE_ROWS, LANES)[:, :, :dec_q, :N_HEADS]
            sel = sel.transpose(1, 3, 2, 0).reshape(-1)
            o = _attn_sample(q, kv_s[0], kv_s[1], sel, cache_k, cache_v, page_table, slopes, n_q=dec_q)
            hs = _oproj(hs, o, j, w, tm=n_s)
        hp = _moe(hp, l, w, tm=512)
        hs = _moe(hs, l, w, tm=n_s)
        hp = _ple(hp, pp, l, w, tm=512)
        hs = _ple(hs, ps, l, w, tm=n_s)
        if l == N_A_LAYERS - 1:
            kv_p = _shared_kv(hp, w, tm=MOBA_BLOCK, blocks_per_seq=blocks_per_seq)
            kv_s = _shared_kv(hs, w, tm=n_s)
            km_p = _block_diag_means(kv_p[4].reshape(bsz, blocks_per_seq, N_HEADS, HEAD_DIM), blocks_per_seq)
            km_s = _block_diag_means(_kmean_cache(cache_k, page_table), n_past_blk)

    heads = (N_HEADS, HEAD_DIM)
    y_prompt = hp.reshape(bsz, seq, D_MODEL)
    y_sample = hs.reshape(dec_b, SAMPLE_ROWS, D_MODEL)[:, :dec_q]
    k_prompt, v_prompt = (t.reshape(bsz, seq, *heads) for t in kv_p[:2])
    k_sample, v_sample = (t.reshape(dec_b, SAMPLE_ROWS, *heads)[:, :dec_q] for t in kv_s[:2])
    return (y_prompt, y_sample, jnp.stack(pool_p), jnp.stack(pool_s), k_prompt, v_prompt, k_sample, v_sample)
```

```python
import functools

import jax
import jax.numpy as jnp
from jax import lax
from jax.experimental import pallas as pl
from jax.experimental.pallas import tpu as pltpu

F32 = jnp.float32
BF16 = jnp.bfloat16

D_MODEL = 1024
DEPTH = 4
N_A_LAYERS = DEPTH // 2
POOL_WINDOWS = (2, 4, 8, 16)
POOL_GROUP_DIM = D_MODEL // len(POOL_WINDOWS)
POOL_STATE = max(POOL_WINDOWS) - 1
POOL_HALO = POOL_STATE + 1
HEAD_DIM = 128
N_HEADS = D_MODEL // HEAD_DIM
MOBA_BLOCK = 256
MOBA_TOP_K = 3
PAGE_SIZE = 128
PAGES_PER_BLOCK = MOBA_BLOCK // PAGE_SIZE
ALIBI_MAX_BIAS = 8.0
N_GROUPS = 4
EXPERTS_PER_GROUP = 4
N_EXPERTS = N_GROUPS * EXPERTS_PER_GROUP
D_EXPERT = D_MODEL // 2
PLE_DIM = 256
NORM_EPS = 1e-6
LANES = 128
SAMPLE_ROWS = 8
NEG = -1e30
VMEM_LIMIT = 48 * 1024 * 1024


def _cparams(*sem):
    return pltpu.CompilerParams(dimension_semantics=sem, vmem_limit_bytes=VMEM_LIMIT)


def _rms(x, g):
    return x * lax.rsqrt(jnp.mean(x * x, axis=-1, keepdims=True) + NORM_EPS) * g


def _dot(a, b):
    return jnp.dot(a, b, preferred_element_type=F32)


def _split(x):
    bits = lax.bitcast_convert_type(x, jnp.uint32) & jnp.uint32(0xFFFF0000)
    hi = lax.bitcast_convert_type(bits, F32)
    return hi.astype(BF16), (x - hi).astype(BF16)


def _dot3(a, b_hi, b_lo):
    a_hi, a_lo = _split(a)
    return _dot(a_hi, b_hi) + (_dot(a_hi, b_lo) + _dot(a_lo, b_hi))


def _sigmoid(x):
    return 1.0 / (1.0 + jnp.exp(-x))


def _pool_kernel(h_ref, pre_ref, g_ref, whi_ref, wlo_ref, sc_ref, o_ref, st_ref, ext_ref, *, ts, n_valid, pos0):
    s = pl.program_id(1)

    @pl.when(s == 0)
    def _():
        ext_ref[0:POOL_HALO, :] = pre_ref[0]

    x = h_ref[0]
    hn = _rms(x, g_ref[...])
    ext_ref[POOL_HALO:POOL_HALO + ts, :] = hn
    pos = pos0 + s * ts + lax.broadcasted_iota(jnp.int32, (ts, 1), 0)
    for g, w in enumerate(POOL_WINDOWS):
        c0, c1 = g * POOL_GROUP_DIM, (g + 1) * POOL_GROUP_DIM
        acc = hn[:, c0:c1]
        for j in range(1, w):
            acc = acc + ext_ref[POOL_HALO - j:POOL_HALO - j + ts, c0:c1]
        cnt = jnp.minimum(pos + 1, w).astype(F32)
        diff = acc / cnt - hn[:, c0:c1]
        y = _dot3(diff, whi_ref[g], wlo_ref[g])
        o_ref[0, :, c0:c1] = x[:, c0:c1] + y * sc_ref[:, c0:c1]
    tail = ext_ref[n_valid:n_valid + POOL_HALO, :]
    st_ref[0] = tail
    ext_ref[0:POOL_HALO, :] = tail


def _pool_mixer(h, prefix, g, w_hi, w_lo, scale, *, ts, n_valid, pos0):
    bsz, seq, _ = h.shape
    kern = functools.partial(_pool_kernel, ts=ts, n_valid=n_valid, pos0=pos0)
    return pl.pallas_call(
        kern,
        out_shape=(jax.ShapeDtypeStruct(h.shape, F32), jax.ShapeDtypeStruct((bsz, POOL_HALO, D_MODEL), F32)),
        grid=(bsz, seq // ts),
        in_specs=[
            pl.BlockSpec((1, ts, D_MODEL), lambda b, s: (b, s, 0)),
            pl.BlockSpec((1, POOL_HALO, D_MODEL), lambda b, s: (b, 0, 0)),
            pl.BlockSpec((1, D_MODEL), lambda b, s: (0, 0)),
            pl.BlockSpec((len(POOL_WINDOWS), POOL_GROUP_DIM, POOL_GROUP_DIM), lambda b, s: (0, 0, 0)),
            pl.BlockSpec((len(POOL_WINDOWS), POOL_GROUP_DIM, POOL_GROUP_DIM), lambda b, s: (0, 0, 0)),
            pl.BlockSpec((1, D_MODEL), lambda b, s: (0, 0)),
        ],
        out_specs=(
            pl.BlockSpec((1, ts, D_MODEL), lambda b, s: (b, s, 0)),
            pl.BlockSpec((1, POOL_HALO, D_MODEL), lambda b, s: (b, 0, 0)),
        ),
        scratch_shapes=[pltpu.VMEM((POOL_HALO + ts, D_MODEL), F32)],
        compiler_params=_cparams("parallel", "arbitrary"),
        name="pool_mixer",
    )(h, prefix, g, w_hi, w_lo, scale)


def _route(xn, wr_hi, wr_lo, br):
    logits = _dot3(xn, wr_hi, wr_lo) + br
    lane = lax.broadcasted_iota(jnp.int32, logits.shape, 1)
    big = jnp.int32(1 << 20)
    lg = jnp.where(lane < N_GROUPS, logits, -jnp.inf)
    mg = jnp.max(lg, axis=-1, keepdims=True)
    g_idx = jnp.min(jnp.where(lg == mg, lane, big), axis=-1, keepdims=True)
    p_grp = 1.0 / jnp.sum(jnp.exp(lg - mg), axis=-1, keepdims=True)
    e0 = N_GROUPS + EXPERTS_PER_GROUP * g_idx
    le = jnp.where((lane >= e0) & (lane < e0 + EXPERTS_PER_GROUP), logits, -jnp.inf)
    m1 = jnp.max(le, axis=-1, keepdims=True)
    i1 = jnp.min(jnp.where(le == m1, lane, big), axis=-1, keepdims=True)
    le2 = jnp.where(lane == i1, -jnp.inf, le)
    m2 = jnp.max(le2, axis=-1, keepdims=True)
    i2 = jnp.min(jnp.where(le2 == m2, lane, big), axis=-1, keepdims=True)
    e2 = jnp.exp(m2 - m1)
    w1 = p_grp / (1.0 + e2)
    w2 = p_grp * e2 / (1.0 + e2)
    return jnp.where(lane == i1, w1, 0.0) + jnp.where(lane == i2, w2, 0.0), g_idx


def _router_kernel(h_ref, g_ref, wrh_ref, wrl_ref, br_ref, xn_ref, grp_ref):
    xn = _rms(h_ref[...], g_ref[...])
    xn_ref[...] = xn
    _, g_idx = _route(xn, wrh_ref[...], wrl_ref[...], br_ref[...])
    grp_ref[...] = jnp.broadcast_to(g_idx, grp_ref.shape)


def _router(h, l, w, *, tm):
    n = h.shape[0]
    return pl.pallas_call(
        _router_kernel,
        out_shape=(jax.ShapeDtypeStruct((n, D_MODEL), F32), jax.ShapeDtypeStruct((n, LANES), jnp.int32)),
        grid=(n // tm,),
        in_specs=[
            pl.BlockSpec((tm, D_MODEL), lambda i: (i, 0)),
            pl.BlockSpec((None, 1, D_MODEL), lambda i: (l, 0, 0)),
            pl.BlockSpec((None, D_MODEL, LANES), lambda i: (l, 0, 0)),
            pl.BlockSpec((None, D_MODEL, LANES), lambda i: (l, 0, 0)),
            pl.BlockSpec((None, 1, LANES), lambda i: (l, 0, 0)),
        ],
        out_specs=(pl.BlockSpec((tm, D_MODEL), lambda i: (i, 0)), pl.BlockSpec((tm, LANES), lambda i: (i, 0))),
        compiler_params=_cparams("parallel"),
        name="router",
    )(h, w["norm_ffn"], w["wr_hi"], w["wr_lo"], w["br"])


ROW_DMA_CHUNK = 64


def _move_rows_kernel(slot_ref, src_ref, *refs, n, scatter):
    dst_ref, sem = refs[-2], refs[-1]

    def issue(c):
        for r in range(ROW_DMA_CHUNK):
            t = c * ROW_DMA_CHUNK + r
            s = slot_ref[t]
            src, dst = (pl.ds(t, 1), pl.ds(s, 1)) if scatter else (pl.ds(s, 1), pl.ds(t, 1))
            pltpu.make_async_copy(src_ref.at[src], dst_ref.at[dst], sem).start()

    def wait_chunk():
        rows = pl.ds(0, ROW_DMA_CHUNK)
        pltpu.make_async_copy(src_ref.at[rows], dst_ref.at[rows], sem).wait()

    issue(0)

    def loop(c, carry):
        issue(c)
        wait_chunk()
        return carry

    lax.fori_loop(1, n // ROW_DMA_CHUNK, loop, 0)
    wait_chunk()


def _scatter_rows(src, slot, n_rows_out):
    n = src.shape[0]
    zeros = jnp.zeros((n_rows_out, src.shape[1]), src.dtype)
    hbm = pl.BlockSpec(memory_space=pl.ANY)
    return pl.pallas_call(
        functools.partial(_move_rows_kernel, n=n, scatter=True),
        out_shape=jax.ShapeDtypeStruct(zeros.shape, zeros.dtype),
        grid_spec=pltpu.PrefetchScalarGridSpec(
            num_scalar_prefetch=1, grid=(1,), in_specs=[hbm, hbm], out_specs=hbm,
            scratch_shapes=[pltpu.SemaphoreType.DMA(())]),
        input_output_aliases={2: 0},
        compiler_params=_cparams("arbitrary"),
        name="scatter_rows",
    )(slot, src, zeros)


def _gather_rows(src, slot):
    n = slot.shape[0]
    hbm = pl.BlockSpec(memory_space=pl.ANY)
    return pl.pallas_call(
        functools.partial(_move_rows_kernel, n=n, scatter=False),
        out_shape=jax.ShapeDtypeStruct((n, src.shape[1]), src.dtype),
        grid_spec=pltpu.PrefetchScalarGridSpec(
            num_scalar_prefetch=1, grid=(1,), in_specs=[hbm], out_specs=hbm,
            scratch_shapes=[pltpu.SemaphoreType.DMA(())]),
        compiler_params=_cparams("arbitrary"),
        name="gather_rows",
    )(slot, src)


def _experts_kernel(grp_ref, live_ref, x_ref, wrh_ref, wrl_ref, br_ref, wg_ref, wu_ref, wd_ref, o_ref,
                    xb_ref, gates_ref, acc_ref):
    i = pl.program_id(0)
    e = pl.program_id(1)
    live = live_ref[i] > 0

    @pl.when(e == 0)
    def _():
        x = x_ref[...]
        xb_ref[...] = x.astype(BF16)
        gates_ref[...] = _route(x, wrh_ref[...], wrl_ref[...], br_ref[...])[0]
        acc_ref[...] = jnp.zeros_like(acc_ref)

    @pl.when(live)
    def _():
        xb = xb_ref[...]
        a = _dot(xb, wg_ref[...])
        u = _dot(xb, wu_ref[...])
        gates = gates_ref[...]
        lane = lax.broadcasted_iota(jnp.int32, gates.shape, 1)
        expert_lane = N_GROUPS + grp_ref[i] * EXPERTS_PER_GROUP + e
        gate = jnp.sum(jnp.where(lane == expert_lane, gates, 0.0), axis=-1, keepdims=True)
        hid = a * _sigmoid(a) * u * gate
        acc_ref[...] += _dot(hid.astype(BF16), wd_ref[...])

    @pl.when(e == EXPERTS_PER_GROUP - 1)
    def _():
        o_ref[...] = acc_ref[...]


def _experts(xs, tile_grp, tile_live, l, w, *, tm):
    n_tiles = xs.shape[0] // tm

    def wspec(shape):
        return pl.BlockSpec((None, None) + shape, lambda i, e, grp, live: (l, grp[i] * EXPERTS_PER_GROUP + e, 0, 0))

    def const(shape):
        return pl.BlockSpec((None,) + shape, lambda i, e, grp, live: (l, 0, 0))

    tok = pl.BlockSpec((tm, D_MODEL), lambda i, e, grp, live: (i, 0))
    return pl.pallas_call(
        _experts_kernel,
        out_shape=jax.ShapeDtypeStruct(xs.shape, F32),
        grid_spec=pltpu.PrefetchScalarGridSpec(
            num_scalar_prefetch=2,
            grid=(n_tiles, EXPERTS_PER_GROUP),
            in_specs=[tok, const((D_MODEL, LANES)), const((D_MODEL, LANES)), const((1, LANES)),
                      wspec((D_MODEL, D_EXPERT)), wspec((D_MODEL, D_EXPERT)), wspec((D_EXPERT, D_MODEL))],
            out_specs=tok,
            scratch_shapes=[
                pltpu.VMEM((tm, D_MODEL), BF16),
                pltpu.VMEM((tm, LANES), F32),
                pltpu.VMEM((tm, D_MODEL), F32),
            ],
        ),
        compiler_params=_cparams("parallel", "arbitrary"),
        name="experts",
    )(tile_grp, tile_live, xs, w["wr_hi"], w["wr_lo"], w["br"], w["w_gate"], w["w_up"], w["w_down"])


def _moe(h, l, w, *, tm):
    n = h.shape[0]
    xn, grp = _router(h, l, w, tm=tm)
    grp = grp[:, 0]
    onehot = (grp[:, None] == jnp.arange(N_GROUPS, dtype=jnp.int32)[None, :]).astype(jnp.int32)
    counts = jnp.sum(onehot, axis=0)
    rank = jnp.sum((jnp.cumsum(onehot, axis=0) - onehot) * onehot, axis=1)
    tiles = (counts + tm - 1) // tm
    tile_end = jnp.cumsum(tiles)
    start = (tile_end - tiles) * tm
    slot = jnp.sum(onehot * start[None, :], axis=1) + rank
    n_tiles = n // tm + N_GROUPS
    tile_id = jnp.arange(n_tiles, dtype=jnp.int32)
    tile_grp = jnp.minimum(jnp.sum((tile_id[:, None] >= tile_end[None, :]).astype(jnp.int32), axis=1), N_GROUPS - 1)
    tile_live = (tile_id < tile_end[-1]).astype(jnp.int32)
    xs = _scatter_rows(xn, slot, n_tiles * tm)
    ys = _experts(xs, tile_grp, tile_live, l, w, tm=tm)
    return _gather_rows(ys, slot)


def _ple_kernel(h_ref, y_ref, p_ref, g_ref, wg_ref, wp_ref, o_ref):
    h = h_ref[...] + y_ref[...]
    hn = _rms(h, g_ref[...]).astype(BF16)
    gate = _sigmoid(_dot(hn, wg_ref[...]))
    o_ref[...] = h + _dot(p_ref[...].astype(BF16), wp_ref[...]) * gate


def _ple(h, y, p, l, w, *, tm):
    n = h.shape[0]
    return pl.pallas_call(
        _ple_kernel,
        out_shape=jax.ShapeDtypeStruct(h.shape, F32),
        grid=(n // tm,),
        in_specs=[
            pl.BlockSpec((tm, D_MODEL), lambda i: (i, 0)),
            pl.BlockSpec((tm, D_MODEL), lambda i: (i, 0)),
            pl.BlockSpec((None, tm, PLE_DIM), lambda i: (l, i, 0)),
            pl.BlockSpec((None, 1, D_MODEL), lambda i: (l, 0, 0)),
            pl.BlockSpec((None, D_MODEL, D_MODEL), lambda i: (l, 0, 0)),
            pl.BlockSpec((None, PLE_DIM, D_MODEL), lambda i: (l, 0, 0)),
        ],
        out_specs=pl.BlockSpec((tm, D_MODEL), lambda i: (i, 0)),
        compiler_params=_cparams("parallel"),
        name="ple",
    )(h, y, p, w["norm_ple"], w["w_ple_gate"], w["w_ple"])


def _head_rms(x, g):
    outs = []
    for hh in range(N_HEADS):
        outs.append(_rms(x[:, hh * HEAD_DIM:(hh + 1) * HEAD_DIM], g))
    return jnp.concatenate(outs, axis=-1)


FEAT_BLOCK_START = 16
FEAT_OFFSET = 17
FEAT_ONES_ROWS = 8


def _alibi_slope(hh):
    return 2.0 ** (-ALIBI_MAX_BIAS * (hh + 1) / N_HEADS)


assert all(_alibi_slope(hh) == 2.0 ** round(-ALIBI_MAX_BIAS * (hh + 1) / N_HEADS) for hh in range(N_HEADS))


def _kv_kernel(h_ref, g_ref, w_ref, kn_ref, k_ref, v_ref, *attn_refs, blocks_per_seq):
    hn = _rms(h_ref[...], g_ref[...]).astype(BF16)
    kv = _dot(hn, w_ref[...])
    k = _head_rms(kv[:, :D_MODEL], kn_ref[...])
    v = kv[:, D_MODEL:]
    k_ref[...] = k
    v_ref[...] = v
    if not attn_refs:
        return
    ka_ref, vt_ref, km_ref = attn_refs
    km_ref[0] = jnp.sum(k, axis=0, keepdims=True) / MOBA_BLOCK
    tm = k.shape[0]
    blk = pl.program_id(0) % blocks_per_seq
    lane = lax.broadcasted_iota(jnp.int32, (tm, HEAD_DIM), 1)
    off = lax.broadcasted_iota(jnp.int32, (tm, HEAD_DIM), 0).astype(F32)
    onehot = jnp.where(lane == blk, 1.0, 0.0)
    start = (blk * MOBA_BLOCK).astype(F32)
    for hh in range(N_HEADS):
        slope = _alibi_slope(hh)
        feat = onehot + jnp.where(lane == FEAT_BLOCK_START, slope * start, 0.0) \
            + jnp.where(lane == FEAT_OFFSET, slope * off, 0.0)
        c0 = hh * 2 * HEAD_DIM
        ka_ref[:, c0:c0 + HEAD_DIM] = k[:, hh * HEAD_DIM:(hh + 1) * HEAD_DIM].astype(BF16)
        ka_ref[:, c0 + HEAD_DIM:c0 + 2 * HEAD_DIM] = feat.astype(BF16)
        vt_ref[hh, 0] = v[:, hh * HEAD_DIM:(hh + 1) * HEAD_DIM].T.astype(BF16)


def _shared_kv(h, w, *, tm, blocks_per_seq=None):
    n = h.shape[0]
    tok = pl.BlockSpec((tm, D_MODEL), lambda i: (i, 0))
    out_shape = [jax.ShapeDtypeStruct((n, D_MODEL), F32), jax.ShapeDtypeStruct((n, D_MODEL), F32)]
    out_specs = [tok, tok]
    if blocks_per_seq is not None:
        out_shape += [
            jax.ShapeDtypeStruct((n, 2 * D_MODEL), BF16),
            jax.ShapeDtypeStruct((N_HEADS, n // tm, HEAD_DIM, tm), BF16),
            jax.ShapeDtypeStruct((n // tm, 1, D_MODEL), F32),
        ]
        out_specs += [
            pl.BlockSpec((tm, 2 * D_MODEL), lambda i: (i, 0)),
            pl.BlockSpec((N_HEADS, 1, HEAD_DIM, tm), lambda i: (0, i, 0, 0)),
            pl.BlockSpec((1, 1, D_MODEL), lambda i: (i, 0, 0)),
        ]
    return pl.pallas_call(
        functools.partial(_kv_kernel, blocks_per_seq=blocks_per_seq),
        out_shape=tuple(out_shape),
        grid=(n // tm,),
        in_specs=[
            tok,
            pl.BlockSpec((1, D_MODEL), lambda i: (0, 0)),
            pl.BlockSpec((D_MODEL, 2 * D_MODEL), lambda i: (0, 0)),
            pl.BlockSpec((1, HEAD_DIM), lambda i: (0, 0)),
        ],
        out_specs=tuple(out_specs),
        compiler_params=_cparams("parallel"),
        name="shared_kv",
    )(h, w["norm_kv"], w["w_kv"], w["k_norm"])


def _rank_in_segments(gv, seg):
    n_lanes = gv.shape[-1]
    lane = lax.broadcasted_iota(jnp.int32, gv.shape, 1)
    n = lane & (seg - 1)
    rank = jnp.zeros(gv.shape, jnp.int32)
    for k in range(1, seg):
        ahead = pltpu.roll(gv, n_lanes - k, axis=1)
        behind = pltpu.roll(gv, seg - k, axis=1)
        wrap = n + k >= seg
        partner = jnp.where(wrap, behind, ahead)
        rank = rank + jnp.where(wrap, (partner >= gv).astype(jnp.int32), (partner > gv).astype(jnp.int32))
    return rank


def _queries(h_ref, g_ref, wq_ref, qn_ref):
    hn = _rms(h_ref[...], g_ref[...]).astype(BF16)
    return _head_rms(_dot(hn, wq_ref[...]), qn_ref[...])


def _q_prompt_kernel(h_ref, g_ref, wq_ref, qn_ref, kmh_ref, kml_ref, qt_ref, *, blocks_per_seq):
    own = pl.program_id(0) % blocks_per_seq
    q = _queries(h_ref, g_ref, wq_ref, qn_ref)
    tm = q.shape[0]
    gate = _dot3(q, kmh_ref[0], kml_ref[0])
    lane = lax.broadcasted_iota(jnp.int32, gate.shape, 1)
    blk = lane & (blocks_per_seq - 1)
    valid = blk < own
    gv = jnp.where(valid, gate, -jnp.inf)
    rank = _rank_in_segments(gv, blocks_per_seq)
    keep = jnp.where(valid, (rank < MOBA_TOP_K).astype(jnp.int32), (blk == own).astype(jnp.int32))
    bias_t = jnp.where(keep > 0, 0.0, NEG).T
    ones = jnp.ones((FEAT_ONES_ROWS, tm), F32)
    zeros = jnp.zeros((HEAD_DIM - blocks_per_seq - FEAT_ONES_ROWS, tm), F32)
    for hh in range(N_HEADS):
        qh = q[:, hh * HEAD_DIM:(hh + 1) * HEAD_DIM] * HEAD_DIM ** -0.5
        feat = jnp.concatenate([bias_t[hh * blocks_per_seq:(hh + 1) * blocks_per_seq], ones, zeros], axis=0)
        qt_ref[hh, 0, 0:HEAD_DIM, :] = qh.T.astype(BF16)
        qt_ref[hh, 0, HEAD_DIM:2 * HEAD_DIM, :] = feat.astype(BF16)


def _q_prompt(h, j, w, km_hi, km_lo, *, blocks_per_seq):
    assert blocks_per_seq == FEAT_BLOCK_START and N_HEADS * blocks_per_seq == LANES
    n = h.shape[0]
    tm = MOBA_BLOCK
    kern = functools.partial(_q_prompt_kernel, blocks_per_seq=blocks_per_seq)
    return pl.pallas_call(
        kern,
        out_shape=jax.ShapeDtypeStruct((N_HEADS, n // tm, 2 * HEAD_DIM, tm), BF16),
        grid=(n // tm,),
        in_specs=[
            pl.BlockSpec((tm, D_MODEL), lambda i: (i, 0)),
            pl.BlockSpec((None, 1, D_MODEL), lambda i: (N_A_LAYERS + j, 0, 0)),
            pl.BlockSpec((None, D_MODEL, D_MODEL), lambda i: (j, 0, 0)),
            pl.BlockSpec((None, 1, HEAD_DIM), lambda i: (j, 0, 0)),
            pl.BlockSpec((1, D_MODEL, LANES), lambda i: (i // blocks_per_seq, 0, 0)),
            pl.BlockSpec((1, D_MODEL, LANES), lambda i: (i // blocks_per_seq, 0, 0)),
        ],
        out_specs=pl.BlockSpec((N_HEADS, 1, 2 * HEAD_DIM, tm), lambda i: (0, i, 0, 0)),
        compiler_params=_cparams("parallel"),
        name="q_prompt",
    )(h, w["norm_mix"], w["w_q"], w["q_norm"], km_hi, km_lo)


def _q_sample_kernel(h_ref, g_ref, wq_ref, qn_ref, kmh_ref, kml_ref, q_ref, idx_ref, *, n_blk):
    q = _queries(h_ref, g_ref, wq_ref, qn_ref)
    q_ref[...] = q * HEAD_DIM ** -0.5
    gate = _dot3(q, kmh_ref[0], kml_ref[0])
    rank = _rank_in_segments(gate, n_blk)
    lane = lax.broadcasted_iota(jnp.int32, gate.shape, 1)
    blk = (lane & (n_blk - 1)).astype(F32)
    row = lax.broadcasted_iota(jnp.int32, (gate.shape[1], LANES), 0)
    col = lax.broadcasted_iota(jnp.int32, (gate.shape[1], LANES), 1)
    seg_of = jnp.where((row >= col * n_blk) & (row < (col + 1) * n_blk), 1.0, 0.0).astype(BF16)
    for r in range(MOBA_TOP_K):
        picked = jnp.where(rank == r, blk, 0.0).astype(BF16)
        idx_ref[r] = _dot(picked, seg_of).astype(jnp.int32)


def _q_sample(h, j, w, km_hi, km_lo, *, n_blk):
    n = h.shape[0]
    tm = SAMPLE_ROWS
    kern = functools.partial(_q_sample_kernel, n_blk=n_blk)
    return pl.pallas_call(
        kern,
        out_shape=(jax.ShapeDtypeStruct((n, D_MODEL), F32), jax.ShapeDtypeStruct((MOBA_TOP_K, n, LANES), jnp.int32)),
        grid=(n // tm,),
        in_specs=[
            pl.BlockSpec((tm, D_MODEL), lambda i: (i, 0)),
            pl.BlockSpec((None, 1, D_MODEL), lambda i: (N_A_LAYERS + j, 0, 0)),
            pl.BlockSpec((None, D_MODEL, D_MODEL), lambda i: (j, 0, 0)),
            pl.BlockSpec((None, 1, HEAD_DIM), lambda i: (j, 0, 0)),
            pl.BlockSpec((1, D_MODEL, N_HEADS * n_blk), lambda i: (i, 0, 0)),
            pl.BlockSpec((1, D_MODEL, N_HEADS * n_blk), lambda i: (i, 0, 0)),
        ],
        out_specs=(pl.BlockSpec((tm, D_MODEL), lambda i: (i, 0)), pl.BlockSpec((MOBA_TOP_K, tm, LANES), lambda i: (0, i, 0))),
        compiler_params=_cparams("parallel"),
        name="q_sample",
    )(h, w["norm_mix"], w["w_q"], w["q_norm"], km_hi, km_lo)


def _block_diag_means(km, n_blk):
    bsz = km.shape[0]
    eye = jnp.eye(N_HEADS, dtype=F32)
    t = km.transpose(0, 2, 3, 1)[:, :, :, None, :] * eye[None, :, None, :, None]
    return _split(t.reshape(bsz, D_MODEL, N_HEADS * n_blk))


HEADS_PER_STEP = 2


def _attn_prompt_kernel(qt_ref, ka_ref, vt_ref, o_ref):
    i = pl.program_id(2)
    blk = MOBA_BLOCK
    rows = lax.broadcasted_iota(jnp.int32, (blk, blk), 0)
    cols = lax.broadcasted_iota(jnp.int32, (blk, blk), 1)
    qts = [qt_ref[hp, 0] for hp in range(HEADS_PER_STEP)]

    def scores(hp, n):
        keys = ka_ref[pl.ds(pl.multiple_of(n * blk, blk), blk), hp * 2 * HEAD_DIM:(hp + 1) * 2 * HEAD_DIM]
        return _dot(keys, qts[hp])

    def absorb(s, vt, m, l, acc):
        m_new = jnp.maximum(m, jnp.max(s, axis=0, keepdims=True))
        alpha = jnp.exp(m - m_new)
        p = jnp.exp(s - m_new)
        l = alpha * l + jnp.sum(p, axis=0, keepdims=True)
        acc = alpha * acc + _dot(vt, p.astype(BF16))
        return m_new, l, acc

    init = tuple(
        (jnp.where(cols >= rows, scores(hp, i), NEG), jnp.full((1, blk), NEG, F32), jnp.zeros((1, blk), F32),
         jnp.zeros((HEAD_DIM, blk), F32))
        for hp in range(HEADS_PER_STEP))

    def body(n, carry):
        prev = jnp.where(n == 0, i, n - 1)
        out = []
        for hp in range(HEADS_PER_STEP):
            s, m, l, acc = carry[hp]
            s_next = scores(hp, n)
            out.append((s_next,) + absorb(s, vt_ref[hp, prev], m, l, acc))
        return tuple(out)

    carry = lax.fori_loop(0, i, body, init)
    last = jnp.where(i == 0, i, i - 1)
    for hp in range(HEADS_PER_STEP):
        s, m, l, acc = carry[hp]
        m, l, acc = absorb(s, vt_ref[hp, last], m, l, acc)
        o_ref[:, hp * HEAD_DIM:(hp + 1) * HEAD_DIM] = (acc / l).T.astype(BF16)


def _attn_prompt(qt, ka, vt, *, bsz, blocks_per_seq):
    n = ka.shape[0]
    seq = blocks_per_seq * MOBA_BLOCK
    hps = HEADS_PER_STEP
    return pl.pallas_call(
        _attn_prompt_kernel,
        out_shape=jax.ShapeDtypeStruct((n, D_MODEL), BF16),
        grid=(bsz, N_HEADS // hps, blocks_per_seq),
        in_specs=[
            pl.BlockSpec((hps, 1, 2 * HEAD_DIM, MOBA_BLOCK), lambda b, h, i: (h, b * blocks_per_seq + i, 0, 0)),
            pl.BlockSpec((seq, hps * 2 * HEAD_DIM), lambda b, h, i: (b, h)),
            pl.BlockSpec((hps, blocks_per_seq, HEAD_DIM, MOBA_BLOCK), lambda b, h, i: (h, b, 0, 0)),
        ],
        out_specs=pl.BlockSpec((MOBA_BLOCK, hps * HEAD_DIM), lambda b, h, i: (b * blocks_per_seq + i, h)),
        compiler_params=_cparams("parallel", "parallel", "arbitrary"),
        name="attn_prompt",
    )(qt, ka, vt)


def _kmean_cache_kernel(pt_ref, *refs):
    page_refs, o_ref = refs[:-1], refs[-1]
    for r in range(len(page_refs) // PAGES_PER_BLOCK):
        tot = jnp.sum(page_refs[2 * r][0], axis=0) + jnp.sum(page_refs[2 * r + 1][0], axis=0)
        o_ref[0, r] = tot / MOBA_BLOCK


def _kmean_cache(cache_k, page_table, *, pages_per_step=8):
    bsz, n_pages = page_table.shape
    steps = n_pages // pages_per_step
    blk_per_step = pages_per_step // PAGES_PER_BLOCK
    pt = page_table.reshape(-1)

    def page_spec(r):
        return pl.BlockSpec((1, PAGE_SIZE, N_HEADS, HEAD_DIM),
                            lambda b, j, pt_ref: (pt_ref[b * n_pages + j * pages_per_step + r], 0, 0, 0))

    out = pl.pallas_call(
        _kmean_cache_kernel,
        out_shape=jax.ShapeDtypeStruct((bsz * steps, blk_per_step, N_HEADS, HEAD_DIM), F32),
        grid_spec=pltpu.PrefetchScalarGridSpec(
            num_scalar_prefetch=1,
            grid=(bsz, steps),
            in_specs=[page_spec(r) for r in range(pages_per_step)],
            out_specs=pl.BlockSpec((1, blk_per_step, N_HEADS, HEAD_DIM), lambda b, j, pt_ref: (b * steps + j, 0, 0, 0)),
        ),
        compiler_params=_cparams("parallel", "arbitrary"),
        name="kmean_cache",
    )(pt, *([cache_k] * pages_per_step))
    return out.reshape(bsz, n_pages // PAGES_PER_BLOCK, N_HEADS, HEAD_DIM)


def _attn_sample_kernel(pt_ref, sel_ref, q_ref, kn_ref, vn_ref, slope_ref, ck_ref, cv_ref, o_ref,
                        kbuf, vbuf, sem, *, n_q, n_pages, past_len):
    b = pl.program_id(0)
    hh = pl.program_id(1)
    sel_base = (b * N_HEADS + hh) * n_q * MOBA_TOP_K

    def copies():
        out = []
        for qi in range(n_q):
            for r in range(MOBA_TOP_K):
                n = sel_ref[sel_base + qi * MOBA_TOP_K + r]
                for pg in range(PAGES_PER_BLOCK):
                    page = pt_ref[b * n_pages + n * PAGES_PER_BLOCK + pg]
                    dst = pl.ds(pg * PAGE_SIZE, PAGE_SIZE)
                    out.append(pltpu.make_async_copy(ck_ref.at[page, :, hh, :], kbuf.at[qi, r, dst, :], sem.at[0]))
                    out.append(pltpu.make_async_copy(cv_ref.at[page, :, hh, :], vbuf.at[qi, r, dst, :], sem.at[1]))
        return out

    cps = copies()
    for cp in cps:
        cp.start()
    for cp in cps:
        cp.wait()

    slope = slope_ref[0][:, 0:1]
    k_new = kn_ref[...]
    v_new = vn_ref[...]
    key_off = lax.broadcasted_iota(jnp.int32, (MOBA_BLOCK, 1), 0)
    new_off = lax.broadcasted_iota(jnp.int32, (SAMPLE_ROWS, 1), 0)
    out_row = lax.broadcasted_iota(jnp.int32, (SAMPLE_ROWS, HEAD_DIM), 0)
    out = jnp.zeros((SAMPLE_ROWS, HEAD_DIM), F32)
    for qi in range(n_q):
        qrow = q_ref[qi:qi + 1, :]
        s_own = jnp.sum(k_new * qrow, axis=-1, keepdims=True) - slope * (qi - new_off).astype(F32)
        s_own = jnp.where(new_off <= qi, s_own, NEG)
        scores = []
        m = jnp.max(s_own, axis=0, keepdims=True)
        for r in range(MOBA_TOP_K):
            n = sel_ref[sel_base + qi * MOBA_TOP_K + r]
            dist = (past_len + qi - n * MOBA_BLOCK - key_off).astype(F32)
            s = jnp.sum(kbuf[qi, r] * qrow, axis=-1, keepdims=True) - slope * dist
            scores.append(s)
            m = jnp.maximum(m, jnp.max(s, axis=0, keepdims=True))
        p_own = jnp.exp(s_own - m)
        l = jnp.sum(p_own, axis=0, keepdims=True)
        acc = jnp.sum(p_own * v_new, axis=0, keepdims=True)
        for r in range(MOBA_TOP_K):
            p = jnp.exp(scores[r] - m)
            l = l + jnp.sum(p, axis=0, keepdims=True)
            acc = acc + jnp.sum(p * vbuf[qi, r], axis=0, keepdims=True)
        out = jnp.where(out_row == qi, acc / l, out)
    o_ref[...] = out.astype(BF16)


def _attn_sample(q, k_new, v_new, sel, cache_k, cache_v, page_table, slopes, *, n_q):
    bsz, n_pages = page_table.shape
    kern = functools.partial(_attn_sample_kernel, n_q=n_q, n_pages=n_pages, past_len=n_pages * PAGE_SIZE)
    tok = pl.BlockSpec((SAMPLE_ROWS, HEAD_DIM), lambda b, h, pt_ref, sel_ref: (b, h))
    return pl.pallas_call(
        kern,
        out_shape=jax.ShapeDtypeStruct(q.shape, BF16),
        grid_spec=pltpu.PrefetchScalarGridSpec(
            num_scalar_prefetch=2,
            grid=(bsz, N_HEADS),
            in_specs=[
                tok, tok, tok,
                pl.BlockSpec((1, 1, LANES), lambda b, h, pt_ref, sel_ref: (h, 0, 0)),
                pl.BlockSpec(memory_space=pl.ANY),
                pl.BlockSpec(memory_space=pl.ANY),
            ],
            out_specs=tok,
            scratch_shapes=[
                pltpu.VMEM((n_q, MOBA_TOP_K, MOBA_BLOCK, HEAD_DIM), F32),
                pltpu.VMEM((n_q, MOBA_TOP_K, MOBA_BLOCK, HEAD_DIM), F32),
                pltpu.SemaphoreType.DMA((2,)),
            ],
        ),
        compiler_params=_cparams("arbitrary", "arbitrary"),
        name="attn_sample",
    )(page_table.reshape(-1), sel, q, k_new, v_new, slopes, cache_k, cache_v)


def _oproj_kernel(h_ref, o_ref, w_ref, out_ref):
    out_ref[...] = h_ref[...] + _dot(o_ref[...], w_ref[...])


def _oproj(h, o, j, w, *, tm):
    n = h.shape[0]
    tok = pl.BlockSpec((tm, D_MODEL), lambda i: (i, 0))
    return pl.pallas_call(
        _oproj_kernel,
        out_shape=jax.ShapeDtypeStruct(h.shape, F32),
        grid=(n // tm,),
        in_specs=[tok, tok, pl.BlockSpec((None, D_MODEL, D_MODEL), lambda i: (j, 0, 0))],
        out_specs=tok,
        compiler_params=_cparams("parallel"),
        name="oproj",
    )(h, o, w["w_o"])


def _prepare_weights(norm_mix, norm_ffn, norm_ple, w_pool, pool_scale, norm_kv, w_kv, k_norm, w_q, q_norm, w_o,
                     router_group_w, router_group_b, router_expert_w, router_expert_b, w_gate, w_up, w_down,
                     w_ple, w_ple_gate):
    pad = LANES - N_GROUPS - N_EXPERTS
    wr = jnp.concatenate([router_group_w, router_expert_w, jnp.zeros((DEPTH, D_MODEL, pad), F32)], axis=-1)
    br = jnp.concatenate([router_group_b, router_expert_b, jnp.zeros((DEPTH, pad), F32)], axis=-1)
    wr_hi, wr_lo = _split(wr)
    wp_hi, wp_lo = _split(w_pool)
    return dict(
        norm_mix=norm_mix.reshape(DEPTH, 1, D_MODEL), norm_ffn=norm_ffn.reshape(DEPTH, 1, D_MODEL),
        norm_ple=norm_ple.reshape(DEPTH, 1, D_MODEL), wp_hi=wp_hi, wp_lo=wp_lo,
        pool_scale=pool_scale.reshape(N_A_LAYERS, 1, D_MODEL), norm_kv=norm_kv.reshape(1, D_MODEL),
        w_kv=w_kv.astype(BF16), k_norm=k_norm.reshape(1, HEAD_DIM), w_q=w_q.astype(BF16),
        q_norm=q_norm.reshape(-1, 1, HEAD_DIM), w_o=w_o.astype(BF16),
        wr_hi=wr_hi, wr_lo=wr_lo, br=br.reshape(DEPTH, 1, LANES),
        w_gate=w_gate.astype(BF16), w_up=w_up.astype(BF16), w_down=w_down.astype(BF16),
        w_ple=w_ple.astype(BF16), w_ple_gate=w_ple_gate.astype(BF16),
    )


def kernel(x_prompt, x_sample, p_prompt, p_sample, state_pool, cache_k, cache_v, page_table, norm_mix, norm_ffn, norm_ple, w_pool, pool_scale, norm_kv, w_kv, k_norm, w_q, q_norm, w_o, router_group_w, router_group_b, router_expert_w, router_expert_b, w_gate, w_up, w_down, w_ple, w_ple_gate):
    w = _prepare_weights(norm_mix, norm_ffn, norm_ple, w_pool, pool_scale, norm_kv, w_kv, k_norm, w_q, q_norm, w_o,
                         router_group_w, router_group_b, router_expert_w, router_expert_b, w_gate, w_up, w_down,
                         w_ple, w_ple_gate)
    slopes = jnp.exp2(-ALIBI_MAX_BIAS * jnp.arange(1, N_HEADS + 1, dtype=F32) / N_HEADS)
    slopes = jnp.broadcast_to(slopes[:, None, None], (N_HEADS, 1, LANES))

    bsz, seq, _ = x_prompt.shape
    dec_b, dec_q, _ = x_sample.shape
    n_pages = page_table.shape[1]
    past_len = n_pages * PAGE_SIZE
    blocks_per_seq = seq // MOBA_BLOCK
    n_past_blk = past_len // MOBA_BLOCK
    n_p, n_s = bsz * seq, dec_b * SAMPLE_ROWS
    row_pad = ((0, 0), (0, 0), (0, SAMPLE_ROWS - dec_q), (0, 0))

    hp = x_prompt
    hs = jnp.pad(x_sample, row_pad[1:])
    pp = p_prompt.reshape(DEPTH, n_p, PLE_DIM)
    ps = jnp.pad(p_sample, row_pad).reshape(DEPTH, n_s, PLE_DIM)
    pre_p = jnp.zeros((bsz, POOL_HALO, D_MODEL), F32)
    pre_s = jnp.pad(state_pool, ((0, 0), (0, 0), (1, 0), (0, 0)))

    pool_p, pool_s = [], []
    kv_p = kv_s = km_p = km_s = None
    for l in range(DEPTH):
        if l < N_A_LAYERS:
            g, sc = w["norm_mix"][l], w["pool_scale"][l]
            hp, st = _pool_mixer(hp.reshape(bsz, seq, D_MODEL), pre_p, g, w["wp_hi"][l], w["wp_lo"][l], sc,
                                 ts=512, n_valid=512, pos0=0)
            pool_p.append(st[:, 1:])
            hs, st = _pool_mixer(hs.reshape(dec_b, SAMPLE_ROWS, D_MODEL), pre_s[l], g, w["wp_hi"][l], w["wp_lo"][l], sc,
                                 ts=SAMPLE_ROWS, n_valid=dec_q, pos0=past_len)
            pool_s.append(st[:, 1:])
            hp, hs = hp.reshape(n_p, D_MODEL), hs.reshape(n_s, D_MODEL)
        else:
            j = l - N_A_LAYERS
            qt = _q_prompt(hp, j, w, *km_p, blocks_per_seq=blocks_per_seq)
            o = _attn_prompt(qt, kv_p[2], kv_p[3], bsz=bsz, blocks_per_seq=blocks_per_seq)
            hp = _oproj(hp, o, j, w, tm=512)
            q, idx = _q_sample(hs, j, w, *km_s, n_blk=n_past_blk)
            sel = idx.reshape(MOBA_TOP_K, dec_b, SAMPLE_ROWS, LANES)[:, :, :dec_q, :N_HEADS]
            sel = sel.transpose(1, 3, 2, 0).reshape(-1)
            o = _attn_sample(q, kv_s[0], kv_s[1], sel, cache_k, cache_v, page_table, slopes, n_q=dec_q)
            hs = _oproj(hs, o, j, w, tm=n_s)
        hp = _ple(hp, _moe(hp, l, w, tm=512), pp, l, w, tm=512)
        hs = _ple(hs, _moe(hs, l, w, tm=n_s), ps, l, w, tm=n_s)
        if l == N_A_LAYERS - 1:
            kv_p = _shared_kv(hp, w, tm=MOBA_BLOCK, blocks_per_seq=blocks_per_seq)
            kv_s = _shared_kv(hs, w, tm=n_s)
            km_p = _block_diag_means(kv_p[4].reshape(bsz, blocks_per_seq, N_HEADS, HEAD_DIM), blocks_per_seq)
            km_s = _block_diag_means(_kmean_cache(cache_k, page_table), n_past_blk)

    heads = (N_HEADS, HEAD_DIM)
    y_prompt = hp.reshape(bsz, seq, D_MODEL)
    y_sample = hs.reshape(dec_b, SAMPLE_ROWS, D_MODEL)[:, :dec_q]
    k_prompt, v_prompt = (t.reshape(bsz, seq, *heads) for t in kv_p[:2])
    k_sample, v_sample = (t.reshape(dec_b, SAMPLE_ROWS, *heads)[:, :dec_q] for t in kv_s[:2])
    return (y_prompt, y_sample, jnp.stack(pool_p), jnp.stack(pool_s), k_prompt, v_prompt, k_sample, v_sample)
```

```python
import functools

import jax
import jax.numpy as jnp
from jax import lax
from jax.experimental import pallas as pl
from jax.experimental.pallas import tpu as pltpu

F32 = jnp.float32
BF16 = jnp.bfloat16

D_MODEL = 1024
DEPTH = 4
N_A_LAYERS = DEPTH // 2
POOL_WINDOWS = (2, 4, 8, 16)
POOL_GROUP_DIM = D_MODEL // len(POOL_WINDOWS)
POOL_STATE = max(POOL_WINDOWS) - 1
POOL_HALO = POOL_STATE + 1
HEAD_DIM = 128
N_HEADS = D_MODEL // HEAD_DIM
MOBA_BLOCK = 256
MOBA_TOP_K = 3
PAGE_SIZE = 128
PAGES_PER_BLOCK = MOBA_BLOCK // PAGE_SIZE
ALIBI_MAX_BIAS = 8.0
N_GROUPS = 4
EXPERTS_PER_GROUP = 4
N_EXPERTS = N_GROUPS * EXPERTS_PER_GROUP
D_EXPERT = D_MODEL // 2
PLE_DIM = 256
NORM_EPS = 1e-6
LANES = 128
SAMPLE_ROWS = 8
NEG = -1e30
VMEM_LIMIT = 48 * 1024 * 1024


def _cparams(*sem):
    return pltpu.CompilerParams(dimension_semantics=sem, vmem_limit_bytes=VMEM_LIMIT)


def _rms(x, g):
    return x * lax.rsqrt(jnp.mean(x * x, axis=-1, keepdims=True) + NORM_EPS) * g


def _dot(a, b):
    return jnp.dot(a, b, preferred_element_type=F32)


def _split(x):
    bits = lax.bitcast_convert_type(x, jnp.uint32) & jnp.uint32(0xFFFF0000)
    hi = lax.bitcast_convert_type(bits, F32)
    return hi.astype(BF16), (x - hi).astype(BF16)


def _dot3(a, b_hi, b_lo):
    a_hi, a_lo = _split(a)
    return _dot(a_hi, b_hi) + (_dot(a_hi, b_lo) + _dot(a_lo, b_hi))


def _sigmoid(x):
    return 1.0 / (1.0 + jnp.exp(-x))


def _pool_kernel(h_ref, pre_ref, g_ref, whi_ref, wlo_ref, sc_ref, o_ref, st_ref, ext_ref, *, ts, n_valid, pos0):
    s = pl.program_id(1)

    @pl.when(s == 0)
    def _():
        ext_ref[0:POOL_HALO, :] = pre_ref[0]

    x = h_ref[0]
    hn = _rms(x, g_ref[...])
    ext_ref[POOL_HALO:POOL_HALO + ts, :] = hn
    pos = pos0 + s * ts + lax.broadcasted_iota(jnp.int32, (ts, 1), 0)
    for g, w in enumerate(POOL_WINDOWS):
        c0, c1 = g * POOL_GROUP_DIM, (g + 1) * POOL_GROUP_DIM
        acc = hn[:, c0:c1]
        for j in range(1, w):
            acc = acc + ext_ref[POOL_HALO - j:POOL_HALO - j + ts, c0:c1]
        cnt = jnp.minimum(pos + 1, w).astype(F32)
        diff = acc / cnt - hn[:, c0:c1]
        y = _dot3(diff, whi_ref[g], wlo_ref[g])
        o_ref[0, :, c0:c1] = x[:, c0:c1] + y * sc_ref[:, c0:c1]
    tail = ext_ref[n_valid:n_valid + POOL_HALO, :]
    st_ref[0] = tail
    ext_ref[0:POOL_HALO, :] = tail


def _pool_mixer(h, prefix, g, w_hi, w_lo, scale, *, ts, n_valid, pos0):
    bsz, seq, _ = h.shape
    kern = functools.partial(_pool_kernel, ts=ts, n_valid=n_valid, pos0=pos0)
    return pl.pallas_call(
        kern,
        out_shape=(jax.ShapeDtypeStruct(h.shape, F32), jax.ShapeDtypeStruct((bsz, POOL_HALO, D_MODEL), F32)),
        grid=(bsz, seq // ts),
        in_specs=[
            pl.BlockSpec((1, ts, D_MODEL), lambda b, s: (b, s, 0)),
            pl.BlockSpec((1, POOL_HALO, D_MODEL), lambda b, s: (b, 0, 0)),
            pl.BlockSpec((1, D_MODEL), lambda b, s: (0, 0)),
            pl.BlockSpec((len(POOL_WINDOWS), POOL_GROUP_DIM, POOL_GROUP_DIM), lambda b, s: (0, 0, 0)),
            pl.BlockSpec((len(POOL_WINDOWS), POOL_GROUP_DIM, POOL_GROUP_DIM), lambda b, s: (0, 0, 0)),
            pl.BlockSpec((1, D_MODEL), lambda b, s: (0, 0)),
        ],
        out_specs=(
            pl.BlockSpec((1, ts, D_MODEL), lambda b, s: (b, s, 0)),
            pl.BlockSpec((1, POOL_HALO, D_MODEL), lambda b, s: (b, 0, 0)),
        ),
        scratch_shapes=[pltpu.VMEM((POOL_HALO + ts, D_MODEL), F32)],
        compiler_params=_cparams("parallel", "arbitrary"),
        name="pool_mixer",
    )(h, prefix, g, w_hi, w_lo, scale)


def _route(xn, wr_hi, wr_lo, br):
    logits = _dot3(xn, wr_hi, wr_lo) + br
    lane = lax.broadcasted_iota(jnp.int32, logits.shape, 1)
    big = jnp.int32(1 << 20)
    lg = jnp.where(lane < N_GROUPS, logits, -jnp.inf)
    mg = jnp.max(lg, axis=-1, keepdims=True)
    g_idx = jnp.min(jnp.where(lg == mg, lane, big), axis=-1, keepdims=True)
    p_grp = 1.0 / jnp.sum(jnp.exp(lg - mg), axis=-1, keepdims=True)
    e0 = N_GROUPS + EXPERTS_PER_GROUP * g_idx
    le = jnp.where((lane >= e0) & (lane < e0 + EXPERTS_PER_GROUP), logits, -jnp.inf)
    m1 = jnp.max(le, axis=-1, keepdims=True)
    i1 = jnp.min(jnp.where(le == m1, lane, big), axis=-1, keepdims=True)
    le2 = jnp.where(lane == i1, -jnp.inf, le)
    m2 = jnp.max(le2, axis=-1, keepdims=True)
    i2 = jnp.min(jnp.where(le2 == m2, lane, big), axis=-1, keepdims=True)
    e2 = jnp.exp(m2 - m1)
    w1 = p_grp / (1.0 + e2)
    w2 = p_grp * e2 / (1.0 + e2)
    return jnp.where(lane == i1, w1, 0.0) + jnp.where(lane == i2, w2, 0.0), g_idx


SLAB = D_MODEL // LANES


def _store_row_major(ref, x):
    for s in range(SLAB):
        ref[pl.ds(s, x.shape[0], stride=SLAB), :] = x[:, s * LANES:(s + 1) * LANES]


def _load_row_major(ref):
    tm = ref.shape[0] // SLAB
    return jnp.concatenate([ref[pl.ds(s, tm, stride=SLAB), :] for s in range(SLAB)], axis=-1)


def _row_major_spec(tm, index_map):
    return pl.BlockSpec((tm * SLAB, LANES), index_map)


def _router_kernel(h_ref, g_ref, wrh_ref, wrl_ref, br_ref, xn_ref, grp_ref):
    xn = _rms(h_ref[...], g_ref[...])
    _store_row_major(xn_ref, xn)
    _, g_idx = _route(xn, wrh_ref[...], wrl_ref[...], br_ref[...])
    grp_ref[...] = jnp.broadcast_to(g_idx, grp_ref.shape)


def _router(h, l, w, *, tm):
    n = h.shape[0]
    return pl.pallas_call(
        _router_kernel,
        out_shape=(jax.ShapeDtypeStruct((n * SLAB, LANES), F32), jax.ShapeDtypeStruct((n, LANES), jnp.int32)),
        grid=(n // tm,),
        in_specs=[
            pl.BlockSpec((tm, D_MODEL), lambda i: (i, 0)),
            pl.BlockSpec((None, 1, D_MODEL), lambda i: (l, 0, 0)),
            pl.BlockSpec((None, D_MODEL, LANES), lambda i: (l, 0, 0)),
            pl.BlockSpec((None, D_MODEL, LANES), lambda i: (l, 0, 0)),
            pl.BlockSpec((None, 1, LANES), lambda i: (l, 0, 0)),
        ],
        out_specs=(_row_major_spec(tm, lambda i: (i, 0)), pl.BlockSpec((tm, LANES), lambda i: (i, 0))),
        compiler_params=_cparams("parallel"),
        name="router",
    )(h, w["norm_ffn"], w["wr_hi"], w["wr_lo"], w["br"])


def _slab(r):
    return pl.ds(pl.multiple_of(r * SLAB, SLAB), SLAB)


def _experts_kernel(grp_ref, live_ref, src_ref, dst_ref, x_hbm, wrh_ref, wrl_ref, br_ref, wg_ref, wu_ref, wd_ref,
                    y_hbm, xbuf, obuf, xb_ref, gates_ref, acc_ref, sem_in, sem_out, *, tm, n_tokens):
    i = pl.program_id(0)
    e = pl.program_id(1)
    last_e = EXPERTS_PER_GROUP - 1
    live = live_ref[i] > 0
    prev_live = (i > 0) & (live_ref[jnp.maximum(i - 1, 0)] > 0)
    cur = i % 2
    part = tm // EXPERTS_PER_GROUP

    def fetch(tile, buf, r0, count):
        for r in range(count):
            tok = src_ref[tile * tm + r0 + r]
            pltpu.make_async_copy(x_hbm.at[_slab(tok)], xbuf.at[buf, _slab(r0 + r)], sem_in.at[buf]).start()

    def wait_fetch(buf):
        pltpu.make_async_copy(x_hbm.at[pl.ds(0, tm * SLAB)], xbuf.at[buf], sem_in.at[buf]).wait()

    def emit(tile, r0, count):
        for r in range(count):
            tok = dst_ref[tile * tm + r0 + r]
            pltpu.make_async_copy(obuf.at[_slab(r0 + r)], y_hbm.at[_slab(tok)], sem_out).start()

    def wait_emit():
        pltpu.make_async_copy(obuf, y_hbm.at[pl.ds(0, tm * SLAB)], sem_out).wait()

    @pl.when((i == 0) & (e == 0))
    def _():
        obuf[...] = jnp.zeros_like(obuf)

        def prime(c, carry):
            fetch(0, 0, c * part, part)
            return carry

        lax.fori_loop(0, EXPERTS_PER_GROUP, prime, 0)

    @pl.when((e == 0) & ((i == 0) | prev_live))
    def _():
        wait_fetch(cur)

    @pl.when((e == 0) & live)
    def _():
        x = _load_row_major(xbuf.at[cur])
        xb_ref[...] = x.astype(BF16)
        gates_ref[...] = _route(x, wrh_ref[...], wrl_ref[...], br_ref[...])[0]
        acc_ref[...] = jnp.zeros_like(acc_ref)

    @pl.when(live)
    def _():
        fetch(i + 1, 1 - cur, e * part, part)
        prev = jnp.maximum(i - 1, 0)
        for r in range(part):
            row = e * part + r
            tok = jnp.where(i > 0, dst_ref[prev * tm + row], n_tokens + row)
            pltpu.make_async_copy(obuf.at[_slab(row)], y_hbm.at[_slab(tok)], sem_out).start()
        xb = xb_ref[...]
        a = _dot(xb, wg_ref[...])
        u = _dot(xb, wu_ref[...])
        gates = gates_ref[...]
        lane = lax.broadcasted_iota(jnp.int32, gates.shape, 1)
        expert_lane = N_GROUPS + grp_ref[i] * EXPERTS_PER_GROUP + e
        gate = jnp.sum(jnp.where(lane == expert_lane, gates, 0.0), axis=-1, keepdims=True)
        hid = a * _sigmoid(a) * u * gate
        acc_ref[...] += _dot(hid.astype(BF16), wd_ref[...])

    @pl.when(prev_live & jnp.logical_not(live))
    def _():
        emit(i - 1, e * part, part)

    @pl.when((e == last_e) & (live | prev_live))
    def _():
        wait_emit()

    @pl.when((e == last_e) & live)
    def _():
        _store_row_major(obuf, acc_ref[...])


def _experts(xn, tile_grp, tile_live, src, dst, l, w, *, tm):
    n = xn.shape[0] // SLAB
    n_tiles = tile_grp.shape[0]

    def wspec(shape):
        return pl.BlockSpec((None, None) + shape, lambda i, e, grp, *_: (l, grp[i] * EXPERTS_PER_GROUP + e, 0, 0))

    def const(shape):
        return pl.BlockSpec((None,) + shape, lambda i, e, *_: (l, 0, 0))

    hbm = pl.BlockSpec(memory_space=pl.ANY)
    return pl.pallas_call(
        functools.partial(_experts_kernel, tm=tm, n_tokens=n),
        out_shape=jax.ShapeDtypeStruct(((n + tm) * SLAB, LANES), F32),
        grid_spec=pltpu.PrefetchScalarGridSpec(
            num_scalar_prefetch=4,
            grid=(n_tiles, EXPERTS_PER_GROUP),
            in_specs=[hbm, const((D_MODEL, LANES)), const((D_MODEL, LANES)), const((1, LANES)),
                      wspec((D_MODEL, D_EXPERT)), wspec((D_MODEL, D_EXPERT)), wspec((D_EXPERT, D_MODEL))],
            out_specs=hbm,
            scratch_shapes=[
                pltpu.VMEM((2, tm * SLAB, LANES), F32),
                pltpu.VMEM((tm * SLAB, LANES), F32),
                pltpu.VMEM((tm, D_MODEL), BF16),
                pltpu.VMEM((tm, LANES), F32),
                pltpu.VMEM((tm, D_MODEL), F32),
                pltpu.SemaphoreType.DMA((2,)),
                pltpu.SemaphoreType.DMA(()),
            ],
        ),
        compiler_params=_cparams("arbitrary", "arbitrary"),
        name="experts",
    )(tile_grp, tile_live, src, dst, xn, w["wr_hi"], w["wr_lo"], w["br"], w["w_gate"], w["w_up"], w["w_down"])


def _moe(h, l, w, *, tm):
    n = h.shape[0]
    xn, grp = _router(h, l, w, tm=tm)
    grp = grp[:, 0]
    onehot = (grp[:, None] == jnp.arange(N_GROUPS, dtype=jnp.int32)[None, :]).astype(jnp.int32)
    counts = jnp.sum(onehot, axis=0)
    rank = jnp.sum((jnp.cumsum(onehot, axis=0) - onehot) * onehot, axis=1)
    tiles = (counts + tm - 1) // tm
    tile_end = jnp.cumsum(tiles)
    start = (tile_end - tiles) * tm
    slot = jnp.sum(onehot * start[None, :], axis=1) + rank
    n_tiles = n // tm + N_GROUPS
    tile_id = jnp.arange(n_tiles, dtype=jnp.int32)
    tile_grp = jnp.minimum(jnp.sum((tile_id[:, None] >= tile_end[None, :]).astype(jnp.int32), axis=1), N_GROUPS - 1)
    tile_live = (tile_id < tile_end[-1]).astype(jnp.int32)
    src = jnp.zeros((n_tiles * tm,), jnp.int32).at[slot].set(jnp.arange(n, dtype=jnp.int32))
    pos = jnp.arange(n_tiles * tm, dtype=jnp.int32)
    in_grp = (jnp.repeat(tile_grp, tm)[:, None] == jnp.arange(N_GROUPS, dtype=jnp.int32)[None, :]).astype(jnp.int32)
    filled = pos - jnp.sum(in_grp * start[None, :], axis=1) < jnp.sum(in_grp * counts[None, :], axis=1)
    dst = jnp.where(filled & (jnp.repeat(tile_live, tm) > 0), src, n + pos % tm)
    return _experts(xn, tile_grp, tile_live, src, dst, l, w, tm=tm)


def _ple_kernel(h_ref, y_ref, p_ref, g_ref, wg_ref, wp_ref, o_ref):
    h = h_ref[...] + _load_row_major(y_ref)
    hn = _rms(h, g_ref[...]).astype(BF16)
    gate = _sigmoid(_dot(hn, wg_ref[...]))
    o_ref[...] = h + _dot(p_ref[...].astype(BF16), wp_ref[...]) * gate


def _ple(h, y, p, l, w, *, tm):
    n = h.shape[0]
    return pl.pallas_call(
        _ple_kernel,
        out_shape=jax.ShapeDtypeStruct(h.shape, F32),
        grid=(n // tm,),
        in_specs=[
            pl.BlockSpec((tm, D_MODEL), lambda i: (i, 0)),
            _row_major_spec(tm, lambda i: (i, 0)),
            pl.BlockSpec((None, tm, PLE_DIM), lambda i: (l, i, 0)),
            pl.BlockSpec((None, 1, D_MODEL), lambda i: (l, 0, 0)),
            pl.BlockSpec((None, D_MODEL, D_MODEL), lambda i: (l, 0, 0)),
            pl.BlockSpec((None, PLE_DIM, D_MODEL), lambda i: (l, 0, 0)),
        ],
        out_specs=pl.BlockSpec((tm, D_MODEL), lambda i: (i, 0)),
        compiler_params=_cparams("parallel"),
        name="ple",
    )(h, y, p, w["norm_ple"], w["w_ple_gate"], w["w_ple"])


def _head_rms(x, g):
    outs = []
    for hh in range(N_HEADS):
        outs.append(_rms(x[:, hh * HEAD_DIM:(hh + 1) * HEAD_DIM], g))
    return jnp.concatenate(outs, axis=-1)


FEAT_BLOCK_START = 16
FEAT_OFFSET = 17
FEAT_ONES_ROWS = 8


def _alibi_slope(hh):
    return 2.0 ** (-ALIBI_MAX_BIAS * (hh + 1) / N_HEADS)


assert all(_alibi_slope(hh) == 2.0 ** round(-ALIBI_MAX_BIAS * (hh + 1) / N_HEADS) for hh in range(N_HEADS))


def _kv_kernel(h_ref, g_ref, w_ref, kn_ref, k_ref, v_ref, *attn_refs, blocks_per_seq):
    hn = _rms(h_ref[...], g_ref[...]).astype(BF16)
    kv = _dot(hn, w_ref[...])
    k = _head_rms(kv[:, :D_MODEL], kn_ref[...])
    v = kv[:, D_MODEL:]
    k_ref[...] = k
    v_ref[...] = v
    if not attn_refs:
        return
    ka_ref, vt_ref, km_ref = attn_refs
    km_ref[0] = jnp.sum(k, axis=0, keepdims=True) / MOBA_BLOCK
    tm = k.shape[0]
    blk = pl.program_id(0) % blocks_per_seq
    lane = lax.broadcasted_iota(jnp.int32, (tm, HEAD_DIM), 1)
    off = lax.broadcasted_iota(jnp.int32, (tm, HEAD_DIM), 0).astype(F32)
    onehot = jnp.where(lane == blk, 1.0, 0.0)
    start = (blk * MOBA_BLOCK).astype(F32)
    for hh in range(N_HEADS):
        slope = _alibi_slope(hh)
        feat = onehot + jnp.where(lane == FEAT_BLOCK_START, slope * start, 0.0) \
            + jnp.where(lane == FEAT_OFFSET, slope * off, 0.0)
        c0 = hh * 2 * HEAD_DIM
        ka_ref[:, c0:c0 + HEAD_DIM] = k[:, hh * HEAD_DIM:(hh + 1) * HEAD_DIM].astype(BF16)
        ka_ref[:, c0 + HEAD_DIM:c0 + 2 * HEAD_DIM] = feat.astype(BF16)
        vt_ref[hh, 0] = v[:, hh * HEAD_DIM:(hh + 1) * HEAD_DIM].T.astype(BF16)


def _shared_kv(h, w, *, tm, blocks_per_seq=None):
    n = h.shape[0]
    tok = pl.BlockSpec((tm, D_MODEL), lambda i: (i, 0))
    out_shape = [jax.ShapeDtypeStruct((n, D_MODEL), F32), jax.ShapeDtypeStruct((n, D_MODEL), F32)]
    out_specs = [tok, tok]
    if blocks_per_seq is not None:
        out_shape += [
            jax.ShapeDtypeStruct((n, 2 * D_MODEL), BF16),
            jax.ShapeDtypeStruct((N_HEADS, n // tm, HEAD_DIM, tm), BF16),
            jax.ShapeDtypeStruct((n // tm, 1, D_MODEL), F32),
        ]
        out_specs += [
            pl.BlockSpec((tm, 2 * D_MODEL), lambda i: (i, 0)),
            pl.BlockSpec((N_HEADS, 1, HEAD_DIM, tm), lambda i: (0, i, 0, 0)),
            pl.BlockSpec((1, 1, D_MODEL), lambda i: (i, 0, 0)),
        ]
    return pl.pallas_call(
        functools.partial(_kv_kernel, blocks_per_seq=blocks_per_seq),
        out_shape=tuple(out_shape),
        grid=(n // tm,),
        in_specs=[
            tok,
            pl.BlockSpec((1, D_MODEL), lambda i: (0, 0)),
            pl.BlockSpec((D_MODEL, 2 * D_MODEL), lambda i: (0, 0)),
            pl.BlockSpec((1, HEAD_DIM), lambda i: (0, 0)),
        ],
        out_specs=tuple(out_specs),
        compiler_params=_cparams("parallel"),
        name="shared_kv",
    )(h, w["norm_kv"], w["w_kv"], w["k_norm"])


def _rank_in_segments(gv, seg):
    n_lanes = gv.shape[-1]
    lane = lax.broadcasted_iota(jnp.int32, gv.shape, 1)
    n = lane & (seg - 1)
    rank = jnp.zeros(gv.shape, jnp.int32)
    for k in range(1, seg):
        ahead = pltpu.roll(gv, n_lanes - k, axis=1)
        behind = pltpu.roll(gv, seg - k, axis=1)
        wrap = n + k >= seg
        partner = jnp.where(wrap, behind, ahead)
        rank = rank + jnp.where(wrap, (partner >= gv).astype(jnp.int32), (partner > gv).astype(jnp.int32))
    return rank


def _queries(h_ref, g_ref, wq_ref, qn_ref):
    hn = _rms(h_ref[...], g_ref[...]).astype(BF16)
    return _head_rms(_dot(hn, wq_ref[...]), qn_ref[...])


def _q_prompt_kernel(h_ref, g_ref, wq_ref, qn_ref, kmh_ref, kml_ref, qt_ref, *, blocks_per_seq):
    own = pl.program_id(0) % blocks_per_seq
    q = _queries(h_ref, g_ref, wq_ref, qn_ref)
    tm = q.shape[0]
    gate = _dot3(q, kmh_ref[0], kml_ref[0])
    lane = lax.broadcasted_iota(jnp.int32, gate.shape, 1)
    blk = lane & (blocks_per_seq - 1)
    valid = blk < own
    gv = jnp.where(valid, gate, -jnp.inf)
    rank = _rank_in_segments(gv, blocks_per_seq)
    keep = jnp.where(valid, (rank < MOBA_TOP_K).astype(jnp.int32), (blk == own).astype(jnp.int32))
    bias_t = jnp.where(keep > 0, 0.0, NEG).T
    ones = jnp.ones((FEAT_ONES_ROWS, tm), F32)
    zeros = jnp.zeros((HEAD_DIM - blocks_per_seq - FEAT_ONES_ROWS, tm), F32)
    for hh in range(N_HEADS):
        qh = q[:, hh * HEAD_DIM:(hh + 1) * HEAD_DIM] * HEAD_DIM ** -0.5
        feat = jnp.concatenate([bias_t[hh * blocks_per_seq:(hh + 1) * blocks_per_seq], ones, zeros], axis=0)
        qt_ref[hh, 0, 0:HEAD_DIM, :] = qh.T.astype(BF16)
        qt_ref[hh, 0, HEAD_DIM:2 * HEAD_DIM, :] = feat.astype(BF16)


def _q_prompt(h, j, w, km_hi, km_lo, *, blocks_per_seq):
    assert blocks_per_seq == FEAT_BLOCK_START and N_HEADS * blocks_per_seq == LANES
    n = h.shape[0]
    tm = MOBA_BLOCK
    kern = functools.partial(_q_prompt_kernel, blocks_per_seq=blocks_per_seq)
    return pl.pallas_call(
        kern,
        out_shape=jax.ShapeDtypeStruct((N_HEADS, n // tm, 2 * HEAD_DIM, tm), BF16),
        grid=(n // tm,),
        in_specs=[
            pl.BlockSpec((tm, D_MODEL), lambda i: (i, 0)),
            pl.BlockSpec((None, 1, D_MODEL), lambda i: (N_A_LAYERS + j, 0, 0)),
            pl.BlockSpec((None, D_MODEL, D_MODEL), lambda i: (j, 0, 0)),
            pl.BlockSpec((None, 1, HEAD_DIM), lambda i: (j, 0, 0)),
            pl.BlockSpec((1, D_MODEL, LANES), lambda i: (i // blocks_per_seq, 0, 0)),
            pl.BlockSpec((1, D_MODEL, LANES), lambda i: (i // blocks_per_seq, 0, 0)),
        ],
        out_specs=pl.BlockSpec((N_HEADS, 1, 2 * HEAD_DIM, tm), lambda i: (0, i, 0, 0)),
        compiler_params=_cparams("parallel"),
        name="q_prompt",
    )(h, w["norm_mix"], w["w_q"], w["q_norm"], km_hi, km_lo)


def _q_sample_kernel(h_ref, g_ref, wq_ref, qn_ref, kmh_ref, kml_ref, q_ref, idx_ref, *, n_blk):
    q = _queries(h_ref, g_ref, wq_ref, qn_ref)
    q_ref[...] = q * HEAD_DIM ** -0.5
    gate = _dot3(q, kmh_ref[0], kml_ref[0])
    rank = _rank_in_segments(gate, n_blk)
    lane = lax.broadcasted_iota(jnp.int32, gate.shape, 1)
    blk = (lane & (n_blk - 1)).astype(F32)
    row = lax.broadcasted_iota(jnp.int32, (gate.shape[1], LANES), 0)
    col = lax.broadcasted_iota(jnp.int32, (gate.shape[1], LANES), 1)
    seg_of = jnp.where((row >= col * n_blk) & (row < (col + 1) * n_blk), 1.0, 0.0).astype(BF16)
    for r in range(MOBA_TOP_K):
        picked = jnp.where(rank == r, blk, 0.0).astype(BF16)
        idx_ref[r] = _dot(picked, seg_of).astype(jnp.int32)


def _q_sample(h, j, w, km_hi, km_lo, *, n_blk):
    n = h.shape[0]
    tm = SAMPLE_ROWS
    kern = functools.partial(_q_sample_kernel, n_blk=n_blk)
    return pl.pallas_call(
        kern,
        out_shape=(jax.ShapeDtypeStruct((n, D_MODEL), F32), jax.ShapeDtypeStruct((MOBA_TOP_K, n, LANES), jnp.int32)),
        grid=(n // tm,),
        in_specs=[
            pl.BlockSpec((tm, D_MODEL), lambda i: (i, 0)),
            pl.BlockSpec((None, 1, D_MODEL), lambda i: (N_A_LAYERS + j, 0, 0)),
            pl.BlockSpec((None, D_MODEL, D_MODEL), lambda i: (j, 0, 0)),
            pl.BlockSpec((None, 1, HEAD_DIM), lambda i: (j, 0, 0)),
            pl.BlockSpec((1, D_MODEL, N_HEADS * n_blk), lambda i: (i, 0, 0)),
            pl.BlockSpec((1, D_MODEL, N_HEADS * n_blk), lambda i: (i, 0, 0)),
        ],
        out_specs=(pl.BlockSpec((tm, D_MODEL), lambda i: (i, 0)), pl.BlockSpec((MOBA_TOP_K, tm, LANES), lambda i: (0, i, 0))),
        compiler_params=_cparams("parallel"),
        name="q_sample",
    )(h, w["norm_mix"], w["w_q"], w["q_norm"], km_hi, km_lo)


def _block_diag_means(km, n_blk):
    bsz = km.shape[0]
    eye = jnp.eye(N_HEADS, dtype=F32)
    t = km.transpose(0, 2, 3, 1)[:, :, :, None, :] * eye[None, :, None, :, None]
    return _split(t.reshape(bsz, D_MODEL, N_HEADS * n_blk))


HEADS_PER_STEP = 2


def _attn_prompt_kernel(qt_ref, ka_ref, vt_ref, o_ref):
    i = pl.program_id(2)
    blk = MOBA_BLOCK
    rows = lax.broadcasted_iota(jnp.int32, (blk, blk), 0)
    cols = lax.broadcasted_iota(jnp.int32, (blk, blk), 1)
    qts = [qt_ref[hp, 0] for hp in range(HEADS_PER_STEP)]

    def scores(hp, n):
        keys = ka_ref[pl.ds(pl.multiple_of(n * blk, blk), blk), hp * 2 * HEAD_DIM:(hp + 1) * 2 * HEAD_DIM]
        return _dot(keys, qts[hp])

    def absorb(s, vt, m, l, acc):
        m_new = jnp.maximum(m, jnp.max(s, axis=0, keepdims=True))
        alpha = jnp.exp(m - m_new)
        p = jnp.exp(s - m_new)
        l = alpha * l + jnp.sum(p, axis=0, keepdims=True)
        acc = alpha * acc + _dot(vt, p.astype(BF16))
        return m_new, l, acc

    init = tuple(
        (jnp.where(cols >= rows, scores(hp, i), NEG), jnp.full((1, blk), NEG, F32), jnp.zeros((1, blk), F32),
         jnp.zeros((HEAD_DIM, blk), F32))
        for hp in range(HEADS_PER_STEP))

    def body(n, carry):
        prev = jnp.where(n == 0, i, n - 1)
        out = []
        for hp in range(HEADS_PER_STEP):
            s, m, l, acc = carry[hp]
            s_next = scores(hp, n)
            out.append((s_next,) + absorb(s, vt_ref[hp, prev], m, l, acc))
        return tuple(out)

    carry = lax.fori_loop(0, i, body, init)
    last = jnp.where(i == 0, i, i - 1)
    for hp in range(HEADS_PER_STEP):
        s, m, l, acc = carry[hp]
        m, l, acc = absorb(s, vt_ref[hp, last], m, l, acc)
        o_ref[:, hp * HEAD_DIM:(hp + 1) * HEAD_DIM] = (acc / l).T.astype(BF16)


def _attn_prompt(qt, ka, vt, *, bsz, blocks_per_seq):
    n = ka.shape[0]
    seq = blocks_per_seq * MOBA_BLOCK
    hps = HEADS_PER_STEP
    return pl.pallas_call(
        _attn_prompt_kernel,
        out_shape=jax.ShapeDtypeStruct((n, D_MODEL), BF16),
        grid=(bsz, N_HEADS // hps, blocks_per_seq),
        in_specs=[
            pl.BlockSpec((hps, 1, 2 * HEAD_DIM, MOBA_BLOCK), lambda b, h, i: (h, b * blocks_per_seq + i, 0, 0)),
            pl.BlockSpec((seq, hps * 2 * HEAD_DIM), lambda b, h, i: (b, h)),
            pl.BlockSpec((hps, blocks_per_seq, HEAD_DIM, MOBA_BLOCK), lambda b, h, i: (h, b, 0, 0)),
        ],
        out_specs=pl.BlockSpec((MOBA_BLOCK, hps * HEAD_DIM), lambda b, h, i: (b * blocks_per_seq + i, h)),
        compiler_params=_cparams("parallel", "parallel", "arbitrary"),
        name="attn_prompt",
    )(qt, ka, vt)


def _kmean_cache_kernel(pt_ref, *refs):
    page_refs, o_ref = refs[:-1], refs[-1]
    for r in range(len(page_refs) // PAGES_PER_BLOCK):
        tot = jnp.sum(page_refs[2 * r][0], axis=0) + jnp.sum(page_refs[2 * r + 1][0], axis=0)
        o_ref[0, r] = tot / MOBA_BLOCK


def _kmean_cache(cache_k, page_table, *, pages_per_step=8):
    bsz, n_pages = page_table.shape
    steps = n_pages // pages_per_step
    blk_per_step = pages_per_step // PAGES_PER_BLOCK
    pt = page_table.reshape(-1)

    def page_spec(r):
        return pl.BlockSpec((1, PAGE_SIZE, N_HEADS, HEAD_DIM),
                            lambda b, j, pt_ref: (pt_ref[b * n_pages + j * pages_per_step + r], 0, 0, 0))

    out = pl.pallas_call(
        _kmean_cache_kernel,
        out_shape=jax.ShapeDtypeStruct((bsz * steps, blk_per_step, N_HEADS, HEAD_DIM), F32),
        grid_spec=pltpu.PrefetchScalarGridSpec(
            num_scalar_prefetch=1,
            grid=(bsz, steps),
            in_specs=[page_spec(r) for r in range(pages_per_step)],
            out_specs=pl.BlockSpec((1, blk_per_step, N_HEADS, HEAD_DIM), lambda b, j, pt_ref: (b * steps + j, 0, 0, 0)),
        ),
        compiler_params=_cparams("parallel", "arbitrary"),
        name="kmean_cache",
    )(pt, *([cache_k] * pages_per_step))
    return out.reshape(bsz, n_pages // PAGES_PER_BLOCK, N_HEADS, HEAD_DIM)


def _attn_sample_kernel(pt_ref, sel_ref, q_ref, kn_ref, vn_ref, slope_ref, ck_ref, cv_ref, o_ref,
                        kbuf, vbuf, sem, *, n_q, n_pages, past_len):
    b = pl.program_id(0)
    hh = pl.program_id(1)
    step = b * N_HEADS + hh
    n_steps = pl.num_programs(0) * N_HEADS
    cur = step % 2
    sel_base = step * n_q * MOBA_TOP_K

    def copies(step_, buf):
        b_, h_ = step_ // N_HEADS, step_ % N_HEADS
        out = []
        for qi in range(n_q):
            for r in range(MOBA_TOP_K):
                n = sel_ref[step_ * n_q * MOBA_TOP_K + qi * MOBA_TOP_K + r]
                for pg in range(PAGES_PER_BLOCK):
                    page = pt_ref[b_ * n_pages + n * PAGES_PER_BLOCK + pg]
                    dst = pl.ds(pg * PAGE_SIZE, PAGE_SIZE)
                    out.append(pltpu.make_async_copy(ck_ref.at[page, :, h_, :], kbuf.at[buf, qi, r, dst, :], sem.at[buf, 0]))
                    out.append(pltpu.make_async_copy(cv_ref.at[page, :, h_, :], vbuf.at[buf, qi, r, dst, :], sem.at[buf, 1]))
        return out

    @pl.when(step == 0)
    def _():
        for cp in copies(step, cur):
            cp.start()

    @pl.when(step + 1 < n_steps)
    def _():
        for cp in copies(step + 1, 1 - cur):
            cp.start()

    for cp in copies(step, cur):
        cp.wait()

    slope = slope_ref[0][:, 0:1]
    k_new = kn_ref[...]
    v_new = vn_ref[...]
    key_off = lax.broadcasted_iota(jnp.int32, (MOBA_BLOCK, 1), 0)
    new_off = lax.broadcasted_iota(jnp.int32, (SAMPLE_ROWS, 1), 0)
    out_row = lax.broadcasted_iota(jnp.int32, (SAMPLE_ROWS, HEAD_DIM), 0)
    out = jnp.zeros((SAMPLE_ROWS, HEAD_DIM), F32)
    for qi in range(n_q):
        qrow = q_ref[qi:qi + 1, :]
        s_own = jnp.sum(k_new * qrow, axis=-1, keepdims=True) - slope * (qi - new_off).astype(F32)
        s_own = jnp.where(new_off <= qi, s_own, NEG)
        scores = []
        m = jnp.max(s_own, axis=0, keepdims=True)
        for r in range(MOBA_TOP_K):
            n = sel_ref[sel_base + qi * MOBA_TOP_K + r]
            dist = (past_len + qi - n * MOBA_BLOCK - key_off).astype(F32)
            s = jnp.sum(kbuf[cur, qi, r] * qrow, axis=-1, keepdims=True) - slope * dist
            scores.append(s)
            m = jnp.maximum(m, jnp.max(s, axis=0, keepdims=True))
        p_own = jnp.exp(s_own - m)
        l = jnp.sum(p_own, axis=0, keepdims=True)
        acc = jnp.sum(p_own * v_new, axis=0, keepdims=True)
        for r in range(MOBA_TOP_K):
            p = jnp.exp(scores[r] - m)
            l = l + jnp.sum(p, axis=0, keepdims=True)
            acc = acc + jnp.sum(p * vbuf[cur, qi, r], axis=0, keepdims=True)
        out = jnp.where(out_row == qi, acc / l, out)
    o_ref[...] = out.astype(BF16)


def _attn_sample(q, k_new, v_new, sel, cache_k, cache_v, page_table, slopes, *, n_q):
    bsz, n_pages = page_table.shape
    kern = functools.partial(_attn_sample_kernel, n_q=n_q, n_pages=n_pages, past_len=n_pages * PAGE_SIZE)
    tok = pl.BlockSpec((SAMPLE_ROWS, HEAD_DIM), lambda b, h, pt_ref, sel_ref: (b, h))
    return pl.pallas_call(
        kern,
        out_shape=jax.ShapeDtypeStruct(q.shape, BF16),
        grid_spec=pltpu.PrefetchScalarGridSpec(
            num_scalar_prefetch=2,
            grid=(bsz, N_HEADS),
            in_specs=[
                tok, tok, tok,
                pl.BlockSpec((1, 1, LANES), lambda b, h, pt_ref, sel_ref: (h, 0, 0)),
                pl.BlockSpec(memory_space=pl.ANY),
                pl.BlockSpec(memory_space=pl.ANY),
            ],
            out_specs=tok,
            scratch_shapes=[
                pltpu.VMEM((2, n_q, MOBA_TOP_K, MOBA_BLOCK, HEAD_DIM), F32),
                pltpu.VMEM((2, n_q, MOBA_TOP_K, MOBA_BLOCK, HEAD_DIM), F32),
                pltpu.SemaphoreType.DMA((2, 2)),
            ],
        ),
        compiler_params=_cparams("arbitrary", "arbitrary"),
        name="attn_sample",
    )(page_table.reshape(-1), sel, q, k_new, v_new, slopes, cache_k, cache_v)


def _oproj_kernel(h_ref, o_ref, w_ref, out_ref):
    out_ref[...] = h_ref[...] + _dot(o_ref[...], w_ref[...])


def _oproj(h, o, j, w, *, tm):
    n = h.shape[0]
    tok = pl.BlockSpec((tm, D_MODEL), lambda i: (i, 0))
    return pl.pallas_call(
        _oproj_kernel,
        out_shape=jax.ShapeDtypeStruct(h.shape, F32),
        grid=(n // tm,),
        in_specs=[tok, tok, pl.BlockSpec((None, D_MODEL, D_MODEL), lambda i: (j, 0, 0))],
        out_specs=tok,
        compiler_params=_cparams("parallel"),
        name="oproj",
    )(h, o, w["w_o"])


def _prepare_weights(norm_mix, norm_ffn, norm_ple, w_pool, pool_scale, norm_kv, w_kv, k_norm, w_q, q_norm, w_o,
                     router_group_w, router_group_b, router_expert_w, router_expert_b, w_gate, w_up, w_down,
                     w_ple, w_ple_gate):
    pad = LANES - N_GROUPS - N_EXPERTS
    wr = jnp.concatenate([router_group_w, router_expert_w, jnp.zeros((DEPTH, D_MODEL, pad), F32)], axis=-1)
    br = jnp.concatenate([router_group_b, router_expert_b, jnp.zeros((DEPTH, pad), F32)], axis=-1)
    wr_hi, wr_lo = _split(wr)
    wp_hi, wp_lo = _split(w_pool)
    return dict(
        norm_mix=norm_mix.reshape(DEPTH, 1, D_MODEL), norm_ffn=norm_ffn.reshape(DEPTH, 1, D_MODEL),
        norm_ple=norm_ple.reshape(DEPTH, 1, D_MODEL), wp_hi=wp_hi, wp_lo=wp_lo,
        pool_scale=pool_scale.reshape(N_A_LAYERS, 1, D_MODEL), norm_kv=norm_kv.reshape(1, D_MODEL),
        w_kv=w_kv.astype(BF16), k_norm=k_norm.reshape(1, HEAD_DIM), w_q=w_q.astype(BF16),
        q_norm=q_norm.reshape(-1, 1, HEAD_DIM), w_o=w_o.astype(BF16),
        wr_hi=wr_hi, wr_lo=wr_lo, br=br.reshape(DEPTH, 1, LANES),
        w_gate=w_gate.astype(BF16), w_up=w_up.astype(BF16), w_down=w_down.astype(BF16),
        w_ple=w_ple.astype(BF16), w_ple_gate=w_ple_gate.astype(BF16),
    )


def kernel(x_prompt, x_sample, p_prompt, p_sample, state_pool, cache_k, cache_v, page_table, norm_mix, norm_ffn, norm_ple, w_pool, pool_scale, norm_kv, w_kv, k_norm, w_q, q_norm, w_o, router_group_w, router_group_b, router_expert_w, router_expert_b, w_gate, w_up, w_down, w_ple, w_ple_gate):
    w = _prepare_weights(norm_mix, norm_ffn, norm_ple, w_pool, pool_scale, norm_kv, w_kv, k_norm, w_q, q_norm, w_o,
                         router_group_w, router_group_b, router_expert_w, router_expert_b, w_gate, w_up, w_down,
                         w_ple, w_ple_gate)
    slopes = jnp.exp2(-ALIBI_MAX_BIAS * jnp.arange(1, N_HEADS + 1, dtype=F32) / N_HEADS)
    slopes = jnp.broadcast_to(slopes[:, None, None], (N_HEADS, 1, LANES))

    bsz, seq, _ = x_prompt.shape
    dec_b, dec_q, _ = x_sample.shape
    n_pages = page_table.shape[1]
    past_len = n_pages * PAGE_SIZE
    blocks_per_seq = seq // MOBA_BLOCK
    n_past_blk = past_len // MOBA_BLOCK
    n_p, n_s = bsz * seq, dec_b * SAMPLE_ROWS
    row_pad = ((0, 0), (0, 0), (0, SAMPLE_ROWS - dec_q), (0, 0))

    hp = x_prompt
    hs = jnp.pad(x_sample, row_pad[1:])
    pp = p_prompt.reshape(DEPTH, n_p, PLE_DIM)
    ps = jnp.pad(p_sample, row_pad).reshape(DEPTH, n_s, PLE_DIM)
    pre_p = jnp.zeros((bsz, POOL_HALO, D_MODEL), F32)
    pre_s = jnp.pad(state_pool, ((0, 0), (0, 0), (1, 0), (0, 0)))

    pool_p, pool_s = [], []
    kv_p = kv_s = km_p = km_s = None
    for l in range(DEPTH):
        if l < N_A_LAYERS:
            g, sc = w["norm_mix"][l], w["pool_scale"][l]
            hp, st = _pool_mixer(hp.reshape(bsz, seq, D_MODEL), pre_p, g, w["wp_hi"][l], w["wp_lo"][l], sc,
                                 ts=512, n_valid=512, pos0=0)
            pool_p.append(st[:, 1:])
            hs, st = _pool_mixer(hs.reshape(dec_b, SAMPLE_ROWS, D_MODEL), pre_s[l], g, w["wp_hi"][l], w["wp_lo"][l], sc,
                                 ts=SAMPLE_ROWS, n_valid=dec_q, pos0=past_len)
            pool_s.append(st[:, 1:])
            hp, hs = hp.reshape(n_p, D_MODEL), hs.reshape(n_s, D_MODEL)
        else:
            j = l - N_A_LAYERS
            qt = _q_prompt(hp, j, w, *km_p, blocks_per_seq=blocks_per_seq)
            o = _attn_prompt(qt, kv_p[2], kv_p[3], bsz=bsz, blocks_per_seq=blocks_per_seq)
            hp = _oproj(hp, o, j, w, tm=512)
            q, idx = _q_sample(hs, j, w, *km_s, n_blk=n_past_blk)
            sel = idx.reshape(MOBA_TOP_K, dec_b, SAMPLE_ROWS, LANES)[:, :, :dec_q, :N_HEADS]
            sel = sel.transpose(1, 3, 2, 0).reshape(-1)
            o = _attn_sample(q, kv_s[0], kv_s[1], sel, cache_k, cache_v, page_table, slopes, n_q=dec_q)
            hs = _oproj(hs, o, j, w, tm=n_s)
        hp = _ple(hp, _moe(hp, l, w, tm=512), pp, l, w, tm=512)
        hs = _ple(hs, _moe(hs, l, w, tm=n_s), ps, l, w, tm=n_s)
        if l == N_A_LAYERS - 1:
            kv_p = _shared_kv(hp, w, tm=MOBA_BLOCK, blocks_per_seq=blocks_per_seq)
            kv_s = _shared_kv(hs, w, tm=n_s)
            km_p = _block_diag_means(kv_p[4].reshape(bsz, blocks_per_seq, N_HEADS, HEAD_DIM), blocks_per_seq)
            km_s = _block_diag_means(_kmean_cache(cache_k, page_table), n_past_blk)

    heads = (N_HEADS, HEAD_DIM)
    y_prompt = hp.reshape(bsz, seq, D_MODEL)
    y_sample = hs.reshape(dec_b, SAMPLE_ROWS, D_MODEL)[:, :dec_q]
    k_prompt, v_prompt = (t.reshape(bsz, seq, *heads) for t in kv_p[:2])
    k_sample, v_sample = (t.reshape(dec_b, SAMPLE_ROWS, *heads)[:, :dec_q] for t in kv_s[:2])
    return (y_prompt, y_sample, jnp.stack(pool_p), jnp.stack(pool_s), k_prompt, v_prompt, k_sample, v_sample)
```

```python
import functools

import jax
import jax.numpy as jnp
from jax import lax
from jax.experimental import pallas as pl
from jax.experimental.pallas import tpu as pltpu

F32 = jnp.float32
BF16 = jnp.bfloat16

D_MODEL = 1024
DEPTH = 4
N_A_LAYERS = DEPTH // 2
POOL_WINDOWS = (2, 4, 8, 16)
POOL_GROUP_DIM = D_MODEL // len(POOL_WINDOWS)
POOL_STATE = max(POOL_WINDOWS) - 1
POOL_HALO = POOL_STATE + 1
HEAD_DIM = 128
N_HEADS = D_MODEL // HEAD_DIM
MOBA_BLOCK = 256
MOBA_TOP_K = 3
PAGE_SIZE = 128
PAGES_PER_BLOCK = MOBA_BLOCK // PAGE_SIZE
ALIBI_MAX_BIAS = 8.0
N_GROUPS = 4
EXPERTS_PER_GROUP = 4
N_EXPERTS = N_GROUPS * EXPERTS_PER_GROUP
D_EXPERT = D_MODEL // 2
PLE_DIM = 256
NORM_EPS = 1e-6
LANES = 128
SAMPLE_ROWS = 8
NEG = -1e30
VMEM_LIMIT = 48 * 1024 * 1024


def _cparams(*sem):
    return pltpu.CompilerParams(dimension_semantics=sem, vmem_limit_bytes=VMEM_LIMIT)


def _rms(x, g):
    return x * lax.rsqrt(jnp.mean(x * x, axis=-1, keepdims=True) + NORM_EPS) * g


def _dot(a, b):
    return jnp.dot(a, b, preferred_element_type=F32)


def _split(x):
    bits = lax.bitcast_convert_type(x, jnp.uint32) & jnp.uint32(0xFFFF0000)
    hi = lax.bitcast_convert_type(bits, F32)
    return hi.astype(BF16), (x - hi).astype(BF16)


def _dot3(a, b_hi, b_lo):
    a_hi, a_lo = _split(a)
    return _dot(a_hi, b_hi) + (_dot(a_hi, b_lo) + _dot(a_lo, b_hi))


def _sigmoid(x):
    return 1.0 / (1.0 + jnp.exp(-x))


def _pool_kernel(h_ref, pre_ref, g_ref, whi_ref, wlo_ref, sc_ref, o_ref, st_ref, ext_ref, *, ts, n_valid, pos0):
    s = pl.program_id(1)

    @pl.when(s == 0)
    def _():
        ext_ref[0:POOL_HALO, :] = pre_ref[0]

    x = h_ref[0]
    hn = _rms(x, g_ref[...])
    ext_ref[POOL_HALO:POOL_HALO + ts, :] = hn
    pos = pos0 + s * ts + lax.broadcasted_iota(jnp.int32, (ts, 1), 0)
    for g, w in enumerate(POOL_WINDOWS):
        c0, c1 = g * POOL_GROUP_DIM, (g + 1) * POOL_GROUP_DIM
        acc = hn[:, c0:c1]
        for j in range(1, w):
            acc = acc + ext_ref[POOL_HALO - j:POOL_HALO - j + ts, c0:c1]
        cnt = jnp.minimum(pos + 1, w).astype(F32)
        diff = acc / cnt - hn[:, c0:c1]
        y = _dot3(diff, whi_ref[g], wlo_ref[g])
        o_ref[0, :, c0:c1] = x[:, c0:c1] + y * sc_ref[:, c0:c1]
    tail = ext_ref[n_valid:n_valid + POOL_HALO, :]
    st_ref[0] = tail
    ext_ref[0:POOL_HALO, :] = tail


def _pool_mixer(h, prefix, g, w_hi, w_lo, scale, *, ts, n_valid, pos0):
    bsz, seq, _ = h.shape
    kern = functools.partial(_pool_kernel, ts=ts, n_valid=n_valid, pos0=pos0)
    return pl.pallas_call(
        kern,
        out_shape=(jax.ShapeDtypeStruct(h.shape, F32), jax.ShapeDtypeStruct((bsz, POOL_HALO, D_MODEL), F32)),
        grid=(bsz, seq // ts),
        in_specs=[
            pl.BlockSpec((1, ts, D_MODEL), lambda b, s: (b, s, 0)),
            pl.BlockSpec((1, POOL_HALO, D_MODEL), lambda b, s: (b, 0, 0)),
            pl.BlockSpec((1, D_MODEL), lambda b, s: (0, 0)),
            pl.BlockSpec((len(POOL_WINDOWS), POOL_GROUP_DIM, POOL_GROUP_DIM), lambda b, s: (0, 0, 0)),
            pl.BlockSpec((len(POOL_WINDOWS), POOL_GROUP_DIM, POOL_GROUP_DIM), lambda b, s: (0, 0, 0)),
            pl.BlockSpec((1, D_MODEL), lambda b, s: (0, 0)),
        ],
        out_specs=(
            pl.BlockSpec((1, ts, D_MODEL), lambda b, s: (b, s, 0)),
            pl.BlockSpec((1, POOL_HALO, D_MODEL), lambda b, s: (b, 0, 0)),
        ),
        scratch_shapes=[pltpu.VMEM((POOL_HALO + ts, D_MODEL), F32)],
        compiler_params=_cparams("parallel", "arbitrary"),
        name="pool_mixer",
    )(h, prefix, g, w_hi, w_lo, scale)


def _route(xn, wr_hi, wr_lo, br):
    logits = _dot3(xn, wr_hi, wr_lo) + br
    lane = lax.broadcasted_iota(jnp.int32, logits.shape, 1)
    big = jnp.int32(1 << 20)
    lg = jnp.where(lane < N_GROUPS, logits, -jnp.inf)
    mg = jnp.max(lg, axis=-1, keepdims=True)
    g_idx = jnp.min(jnp.where(lg == mg, lane, big), axis=-1, keepdims=True)
    p_grp = 1.0 / jnp.sum(jnp.exp(lg - mg), axis=-1, keepdims=True)
    e0 = N_GROUPS + EXPERTS_PER_GROUP * g_idx
    le = jnp.where((lane >= e0) & (lane < e0 + EXPERTS_PER_GROUP), logits, -jnp.inf)
    m1 = jnp.max(le, axis=-1, keepdims=True)
    i1 = jnp.min(jnp.where(le == m1, lane, big), axis=-1, keepdims=True)
    le2 = jnp.where(lane == i1, -jnp.inf, le)
    m2 = jnp.max(le2, axis=-1, keepdims=True)
    i2 = jnp.min(jnp.where(le2 == m2, lane, big), axis=-1, keepdims=True)
    e2 = jnp.exp(m2 - m1)
    w1 = p_grp / (1.0 + e2)
    w2 = p_grp * e2 / (1.0 + e2)
    return jnp.where(lane == i1, w1, 0.0) + jnp.where(lane == i2, w2, 0.0), g_idx


SLAB = D_MODEL // LANES


def _store_row_major(ref, x):
    for s in range(SLAB):
        ref[pl.ds(s, x.shape[0], stride=SLAB), :] = x[:, s * LANES:(s + 1) * LANES]


def _load_row_major(ref):
    tm = ref.shape[0] // SLAB
    return jnp.concatenate([ref[pl.ds(s, tm, stride=SLAB), :] for s in range(SLAB)], axis=-1)


def _row_major_spec(tm, index_map):
    return pl.BlockSpec((tm * SLAB, LANES), index_map)


def _router_kernel(ha_ref, hb_ref, g_ref, wrh_ref, wrl_ref, br_ref, xn_ref, grp_ref, *, tiles_a):
    h = jnp.where(pl.program_id(0) < tiles_a, ha_ref[...], hb_ref[...])
    xn = _rms(h, g_ref[...])
    _store_row_major(xn_ref, xn)
    _, g_idx = _route(xn, wrh_ref[...], wrl_ref[...], br_ref[...])
    grp_ref[...] = jnp.broadcast_to(g_idx, grp_ref.shape)


def _router(ha, hb, l, w, *, tm):
    tiles_a, tiles_b = ha.shape[0] // tm, hb.shape[0] // tm
    n = ha.shape[0] + hb.shape[0]
    return pl.pallas_call(
        functools.partial(_router_kernel, tiles_a=tiles_a),
        out_shape=(jax.ShapeDtypeStruct((n * SLAB, LANES), F32), jax.ShapeDtypeStruct((n, LANES), jnp.int32)),
        grid=(tiles_a + tiles_b,),
        in_specs=[
            pl.BlockSpec((tm, D_MODEL), lambda i: (jnp.minimum(i, tiles_a - 1), 0)),
            pl.BlockSpec((tm, D_MODEL), lambda i: (jnp.maximum(i - tiles_a, 0), 0)),
            pl.BlockSpec((None, 1, D_MODEL), lambda i: (l, 0, 0)),
            pl.BlockSpec((None, D_MODEL, LANES), lambda i: (l, 0, 0)),
            pl.BlockSpec((None, D_MODEL, LANES), lambda i: (l, 0, 0)),
            pl.BlockSpec((None, 1, LANES), lambda i: (l, 0, 0)),
        ],
        out_specs=(_row_major_spec(tm, lambda i: (i, 0)), pl.BlockSpec((tm, LANES), lambda i: (i, 0))),
        compiler_params=_cparams("parallel"),
        name="router",
    )(ha, hb, w["norm_ffn"], w["wr_hi"], w["wr_lo"], w["br"])


def _slab(r):
    return pl.ds(pl.multiple_of(r * SLAB, SLAB), SLAB)


def _experts_kernel(grp_ref, live_ref, src_ref, dst_ref, x_hbm, wrh_ref, wrl_ref, br_ref, wg_ref, wu_ref, wd_ref,
                    y_hbm, xbuf, obuf, xb_ref, gates_ref, acc_ref, sem_in, sem_out, *, tm, n_tokens):
    i = pl.program_id(0)
    e = pl.program_id(1)
    last_e = EXPERTS_PER_GROUP - 1
    live = live_ref[i] > 0
    prev_live = (i > 0) & (live_ref[jnp.maximum(i - 1, 0)] > 0)
    cur = i % 2
    part = tm // EXPERTS_PER_GROUP

    def fetch(tile, buf, r0, count):
        for r in range(count):
            tok = src_ref[tile * tm + r0 + r]
            pltpu.make_async_copy(x_hbm.at[_slab(tok)], xbuf.at[buf, _slab(r0 + r)], sem_in.at[buf]).start()

    def wait_fetch(buf):
        pltpu.make_async_copy(x_hbm.at[pl.ds(0, tm * SLAB)], xbuf.at[buf], sem_in.at[buf]).wait()

    def emit(tile, r0, count):
        for r in range(count):
            tok = dst_ref[tile * tm + r0 + r]
            pltpu.make_async_copy(obuf.at[_slab(r0 + r)], y_hbm.at[_slab(tok)], sem_out).start()

    def wait_emit():
        pltpu.make_async_copy(obuf, y_hbm.at[pl.ds(0, tm * SLAB)], sem_out).wait()

    @pl.when((i == 0) & (e == 0))
    def _():
        obuf[...] = jnp.zeros_like(obuf)

        def prime(c, carry):
            fetch(0, 0, c * part, part)
            return carry

        lax.fori_loop(0, EXPERTS_PER_GROUP, prime, 0)

    @pl.when((e == 0) & ((i == 0) | prev_live))
    def _():
        wait_fetch(cur)

    @pl.when((e == 0) & live)
    def _():
        x = _load_row_major(xbuf.at[cur])
        xb_ref[...] = x.astype(BF16)
        gates_ref[...] = _route(x, wrh_ref[...], wrl_ref[...], br_ref[...])[0]
        acc_ref[...] = jnp.zeros_like(acc_ref)

    @pl.when(live)
    def _():
        fetch(i + 1, 1 - cur, e * part, part)
        prev = jnp.maximum(i - 1, 0)
        for r in range(part):
            row = e * part + r
            tok = jnp.where(i > 0, dst_ref[prev * tm + row], n_tokens + row)
            pltpu.make_async_copy(obuf.at[_slab(row)], y_hbm.at[_slab(tok)], sem_out).start()
        xb = xb_ref[...]
        a = _dot(xb, wg_ref[...])
        u = _dot(xb, wu_ref[...])
        gates = gates_ref[...]
        lane = lax.broadcasted_iota(jnp.int32, gates.shape, 1)
        expert_lane = N_GROUPS + grp_ref[i] * EXPERTS_PER_GROUP + e
        gate = jnp.sum(jnp.where(lane == expert_lane, gates, 0.0), axis=-1, keepdims=True)
        hid = a * _sigmoid(a) * u * gate
        acc_ref[...] += _dot(hid.astype(BF16), wd_ref[...])

    @pl.when(prev_live & jnp.logical_not(live))
    def _():
        emit(i - 1, e * part, part)

    @pl.when((e == last_e) & (live | prev_live))
    def _():
        wait_emit()

    @pl.when((e == last_e) & live)
    def _():
        _store_row_major(obuf, acc_ref[...])


def _experts(xn, tile_grp, tile_live, src, dst, l, w, *, tm):
    n = xn.shape[0] // SLAB
    n_tiles = tile_grp.shape[0]

    def wspec(shape):
        return pl.BlockSpec((None, None) + shape, lambda i, e, grp, *_: (l, grp[i] * EXPERTS_PER_GROUP + e, 0, 0))

    def const(shape):
        return pl.BlockSpec((None,) + shape, lambda i, e, *_: (l, 0, 0))

    hbm = pl.BlockSpec(memory_space=pl.ANY)
    return pl.pallas_call(
        functools.partial(_experts_kernel, tm=tm, n_tokens=n),
        out_shape=jax.ShapeDtypeStruct(((n + tm) * SLAB, LANES), F32),
        grid_spec=pltpu.PrefetchScalarGridSpec(
            num_scalar_prefetch=4,
            grid=(n_tiles, EXPERTS_PER_GROUP),
            in_specs=[hbm, const((D_MODEL, LANES)), const((D_MODEL, LANES)), const((1, LANES)),
                      wspec((D_MODEL, D_EXPERT)), wspec((D_MODEL, D_EXPERT)), wspec((D_EXPERT, D_MODEL))],
            out_specs=hbm,
            scratch_shapes=[
                pltpu.VMEM((2, tm * SLAB, LANES), F32),
                pltpu.VMEM((tm * SLAB, LANES), F32),
                pltpu.VMEM((tm, D_MODEL), BF16),
                pltpu.VMEM((tm, LANES), F32),
                pltpu.VMEM((tm, D_MODEL), F32),
                pltpu.SemaphoreType.DMA((2,)),
                pltpu.SemaphoreType.DMA(()),
            ],
        ),
        compiler_params=_cparams("arbitrary", "arbitrary"),
        name="experts",
    )(tile_grp, tile_live, src, dst, xn, w["wr_hi"], w["wr_lo"], w["br"], w["w_gate"], w["w_up"], w["w_down"])


ROUTER_TILE = 128
EXPERT_TILES = (640, 512, 384, 256, 128)


def _moe(ha, hb, l, w):
    n = ha.shape[0] + hb.shape[0]
    tm = next(t for t in EXPERT_TILES if n % t == 0)
    xn, grp = _router(ha, hb, l, w, tm=ROUTER_TILE)
    grp = grp[:, 0]
    onehot = (grp[:, None] == jnp.arange(N_GROUPS, dtype=jnp.int32)[None, :]).astype(jnp.int32)
    counts = jnp.sum(onehot, axis=0)
    rank = jnp.sum((jnp.cumsum(onehot, axis=0) - onehot) * onehot, axis=1)
    tiles = (counts + tm - 1) // tm
    tile_end = jnp.cumsum(tiles)
    start = (tile_end - tiles) * tm
    slot = jnp.sum(onehot * start[None, :], axis=1) + rank
    n_tiles = n // tm + N_GROUPS
    tile_id = jnp.arange(n_tiles, dtype=jnp.int32)
    tile_grp = jnp.minimum(jnp.sum((tile_id[:, None] >= tile_end[None, :]).astype(jnp.int32), axis=1), N_GROUPS - 1)
    tile_live = (tile_id < tile_end[-1]).astype(jnp.int32)
    src = jnp.zeros((n_tiles * tm,), jnp.int32).at[slot].set(jnp.arange(n, dtype=jnp.int32))
    pos = jnp.arange(n_tiles * tm, dtype=jnp.int32)
    in_grp = (jnp.repeat(tile_grp, tm)[:, None] == jnp.arange(N_GROUPS, dtype=jnp.int32)[None, :]).astype(jnp.int32)
    filled = pos - jnp.sum(in_grp * start[None, :], axis=1) < jnp.sum(in_grp * counts[None, :], axis=1)
    dst = jnp.where(filled & (jnp.repeat(tile_live, tm) > 0), src, n + pos % tm)
    return _experts(xn, tile_grp, tile_live, src, dst, l, w, tm=tm)


def _ple_kernel(h_ref, y_ref, p_ref, g_ref, wg_ref, wp_ref, o_ref):
    h = h_ref[...] + _load_row_major(y_ref)
    hn = _rms(h, g_ref[...]).astype(BF16)
    gate = _sigmoid(_dot(hn, wg_ref[...]))
    o_ref[...] = h + _dot(p_ref[...].astype(BF16), wp_ref[...]) * gate


def _ple(h, y, p, l, w, *, tm, y_row0=0):
    y_tile0 = y_row0 // tm
    assert y_tile0 * tm == y_row0
    n = h.shape[0]
    return pl.pallas_call(
        _ple_kernel,
        out_shape=jax.ShapeDtypeStruct(h.shape, F32),
        grid=(n // tm,),
        in_specs=[
            pl.BlockSpec((tm, D_MODEL), lambda i: (i, 0)),
            _row_major_spec(tm, lambda i: (i + y_tile0, 0)),
            pl.BlockSpec((None, tm, PLE_DIM), lambda i: (l, i, 0)),
            pl.BlockSpec((None, 1, D_MODEL), lambda i: (l, 0, 0)),
            pl.BlockSpec((None, D_MODEL, D_MODEL), lambda i: (l, 0, 0)),
            pl.BlockSpec((None, PLE_DIM, D_MODEL), lambda i: (l, 0, 0)),
        ],
        out_specs=pl.BlockSpec((tm, D_MODEL), lambda i: (i, 0)),
        compiler_params=_cparams("parallel"),
        name="ple",
    )(h, y, p, w["norm_ple"], w["w_ple_gate"], w["w_ple"])


def _head_rms(x, g):
    outs = []
    for hh in range(N_HEADS):
        outs.append(_rms(x[:, hh * HEAD_DIM:(hh + 1) * HEAD_DIM], g))
    return jnp.concatenate(outs, axis=-1)


FEAT_BLOCK_START = 16
FEAT_OFFSET = 17
FEAT_ONES_ROWS = 8


def _alibi_slope(hh):
    return 2.0 ** (-ALIBI_MAX_BIAS * (hh + 1) / N_HEADS)


assert all(_alibi_slope(hh) == 2.0 ** round(-ALIBI_MAX_BIAS * (hh + 1) / N_HEADS) for hh in range(N_HEADS))


def _kv_kernel(h_ref, g_ref, w_ref, kn_ref, k_ref, v_ref, *attn_refs, blocks_per_seq):
    hn = _rms(h_ref[...], g_ref[...]).astype(BF16)
    kv = _dot(hn, w_ref[...])
    k = _head_rms(kv[:, :D_MODEL], kn_ref[...])
    v = kv[:, D_MODEL:]
    k_ref[...] = k
    v_ref[...] = v
    if not attn_refs:
        return
    ka_ref, vt_ref, km_ref = attn_refs
    km_ref[0] = jnp.sum(k, axis=0, keepdims=True) / MOBA_BLOCK
    tm = k.shape[0]
    blk = pl.program_id(0) % blocks_per_seq
    lane = lax.broadcasted_iota(jnp.int32, (tm, HEAD_DIM), 1)
    off = lax.broadcasted_iota(jnp.int32, (tm, HEAD_DIM), 0).astype(F32)
    onehot = jnp.where(lane == blk, 1.0, 0.0)
    start = (blk * MOBA_BLOCK).astype(F32)
    for hh in range(N_HEADS):
        slope = _alibi_slope(hh)
        feat = onehot + jnp.where(lane == FEAT_BLOCK_START, slope * start, 0.0) \
            + jnp.where(lane == FEAT_OFFSET, slope * off, 0.0)
        c0 = hh * 2 * HEAD_DIM
        ka_ref[:, c0:c0 + HEAD_DIM] = k[:, hh * HEAD_DIM:(hh + 1) * HEAD_DIM].astype(BF16)
        ka_ref[:, c0 + HEAD_DIM:c0 + 2 * HEAD_DIM] = feat.astype(BF16)
        vt_ref[hh, 0] = v[:, hh * HEAD_DIM:(hh + 1) * HEAD_DIM].T.astype(BF16)


def _shared_kv(h, w, *, tm, blocks_per_seq=None):
    n = h.shape[0]
    tok = pl.BlockSpec((tm, D_MODEL), lambda i: (i, 0))
    out_shape = [jax.ShapeDtypeStruct((n, D_MODEL), F32), jax.ShapeDtypeStruct((n, D_MODEL), F32)]
    out_specs = [tok, tok]
    if blocks_per_seq is not None:
        out_shape += [
            jax.ShapeDtypeStruct((n, 2 * D_MODEL), BF16),
            jax.ShapeDtypeStruct((N_HEADS, n // tm, HEAD_DIM, tm), BF16),
            jax.ShapeDtypeStruct((n // tm, 1, D_MODEL), F32),
        ]
        out_specs += [
            pl.BlockSpec((tm, 2 * D_MODEL), lambda i: (i, 0)),
            pl.BlockSpec((N_HEADS, 1, HEAD_DIM, tm), lambda i: (0, i, 0, 0)),
            pl.BlockSpec((1, 1, D_MODEL), lambda i: (i, 0, 0)),
        ]
    return pl.pallas_call(
        functools.partial(_kv_kernel, blocks_per_seq=blocks_per_seq),
        out_shape=tuple(out_shape),
        grid=(n // tm,),
        in_specs=[
            tok,
            pl.BlockSpec((1, D_MODEL), lambda i: (0, 0)),
            pl.BlockSpec((D_MODEL, 2 * D_MODEL), lambda i: (0, 0)),
            pl.BlockSpec((1, HEAD_DIM), lambda i: (0, 0)),
        ],
        out_specs=tuple(out_specs),
        compiler_params=_cparams("parallel"),
        name="shared_kv",
    )(h, w["norm_kv"], w["w_kv"], w["k_norm"])


def _rank_in_segments(gv, seg):
    n_lanes = gv.shape[-1]
    lane = lax.broadcasted_iota(jnp.int32, gv.shape, 1)
    n = lane & (seg - 1)
    rank = jnp.zeros(gv.shape, jnp.int32)
    for k in range(1, seg):
        ahead = pltpu.roll(gv, n_lanes - k, axis=1)
        behind = pltpu.roll(gv, seg - k, axis=1)
        wrap = n + k >= seg
        partner = jnp.where(wrap, behind, ahead)
        rank = rank + jnp.where(wrap, (partner >= gv).astype(jnp.int32), (partner > gv).astype(jnp.int32))
    return rank


def _queries(h_ref, g_ref, wq_ref, qn_ref):
    hn = _rms(h_ref[...], g_ref[...]).astype(BF16)
    return _head_rms(_dot(hn, wq_ref[...]), qn_ref[...])


def _q_prompt_kernel(h_ref, g_ref, wq_ref, qn_ref, kmh_ref, kml_ref, qt_ref, *, blocks_per_seq):
    own = pl.program_id(0) % blocks_per_seq
    q = _queries(h_ref, g_ref, wq_ref, qn_ref)
    tm = q.shape[0]
    gate = _dot3(q, kmh_ref[0], kml_ref[0])
    lane = lax.broadcasted_iota(jnp.int32, gate.shape, 1)
    blk = lane & (blocks_per_seq - 1)
    valid = blk < own
    gv = jnp.where(valid, gate, -jnp.inf)
    rank = _rank_in_segments(gv, blocks_per_seq)
    keep = jnp.where(valid, (rank < MOBA_TOP_K).astype(jnp.int32), (blk == own).astype(jnp.int32))
    bias_t = jnp.where(keep > 0, 0.0, NEG).T
    ones = jnp.ones((FEAT_ONES_ROWS, tm), F32)
    zeros = jnp.zeros((HEAD_DIM - blocks_per_seq - FEAT_ONES_ROWS, tm), F32)
    for hh in range(N_HEADS):
        qh = q[:, hh * HEAD_DIM:(hh + 1) * HEAD_DIM] * HEAD_DIM ** -0.5
        feat = jnp.concatenate([bias_t[hh * blocks_per_seq:(hh + 1) * blocks_per_seq], ones, zeros], axis=0)
        qt_ref[hh, 0, 0:HEAD_DIM, :] = qh.T.astype(BF16)
        qt_ref[hh, 0, HEAD_DIM:2 * HEAD_DIM, :] = feat.astype(BF16)


def _q_prompt(h, j, w, km_hi, km_lo, *, blocks_per_seq):
    assert blocks_per_seq == FEAT_BLOCK_START and N_HEADS * blocks_per_seq == LANES
    n = h.shape[0]
    tm = MOBA_BLOCK
    kern = functools.partial(_q_prompt_kernel, blocks_per_seq=blocks_per_seq)
    return pl.pallas_call(
        kern,
        out_shape=jax.ShapeDtypeStruct((N_HEADS, n // tm, 2 * HEAD_DIM, tm), BF16),
        grid=(n // tm,),
        in_specs=[
            pl.BlockSpec((tm, D_MODEL), lambda i: (i, 0)),
            pl.BlockSpec((None, 1, D_MODEL), lambda i: (N_A_LAYERS + j, 0, 0)),
            pl.BlockSpec((None, D_MODEL, D_MODEL), lambda i: (j, 0, 0)),
            pl.BlockSpec((None, 1, HEAD_DIM), lambda i: (j, 0, 0)),
            pl.BlockSpec((1, D_MODEL, LANES), lambda i: (i // blocks_per_seq, 0, 0)),
            pl.BlockSpec((1, D_MODEL, LANES), lambda i: (i // blocks_per_seq, 0, 0)),
        ],
        out_specs=pl.BlockSpec((N_HEADS, 1, 2 * HEAD_DIM, tm), lambda i: (0, i, 0, 0)),
        compiler_params=_cparams("parallel"),
        name="q_prompt",
    )(h, w["norm_mix"], w["w_q"], w["q_norm"], km_hi, km_lo)


def _q_sample_kernel(h_ref, g_ref, wq_ref, qn_ref, kmh_ref, kml_ref, q_ref, idx_ref, *, n_blk):
    q = _queries(h_ref, g_ref, wq_ref, qn_ref)
    q_ref[...] = q * HEAD_DIM ** -0.5
    gate = _dot3(q, kmh_ref[0], kml_ref[0])
    rank = _rank_in_segments(gate, n_blk)
    lane = lax.broadcasted_iota(jnp.int32, gate.shape, 1)
    blk = (lane & (n_blk - 1)).astype(F32)
    row = lax.broadcasted_iota(jnp.int32, (gate.shape[1], LANES), 0)
    col = lax.broadcasted_iota(jnp.int32, (gate.shape[1], LANES), 1)
    seg_of = jnp.where((row >= col * n_blk) & (row < (col + 1) * n_blk), 1.0, 0.0).astype(BF16)
    for r in range(MOBA_TOP_K):
        picked = jnp.where(rank == r, blk, 0.0).astype(BF16)
        idx_ref[r] = _dot(picked, seg_of).astype(jnp.int32)


def _q_sample(h, j, w, km_hi, km_lo, *, n_blk):
    n = h.shape[0]
    tm = SAMPLE_ROWS
    kern = functools.partial(_q_sample_kernel, n_blk=n_blk)
    return pl.pallas_call(
        kern,
        out_shape=(jax.ShapeDtypeStruct((n, D_MODEL), F32), jax.ShapeDtypeStruct((MOBA_TOP_K, n, LANES), jnp.int32)),
        grid=(n // tm,),
        in_specs=[
            pl.BlockSpec((tm, D_MODEL), lambda i: (i, 0)),
            pl.BlockSpec((None, 1, D_MODEL), lambda i: (N_A_LAYERS + j, 0, 0)),
            pl.BlockSpec((None, D_MODEL, D_MODEL), lambda i: (j, 0, 0)),
            pl.BlockSpec((None, 1, HEAD_DIM), lambda i: (j, 0, 0)),
            pl.BlockSpec((1, D_MODEL, N_HEADS * n_blk), lambda i: (i, 0, 0)),
            pl.BlockSpec((1, D_MODEL, N_HEADS * n_blk), lambda i: (i, 0, 0)),
        ],
        out_specs=(pl.BlockSpec((tm, D_MODEL), lambda i: (i, 0)), pl.BlockSpec((MOBA_TOP_K, tm, LANES), lambda i: (0, i, 0))),
        compiler_params=_cparams("parallel"),
        name="q_sample",
    )(h, w["norm_mix"], w["w_q"], w["q_norm"], km_hi, km_lo)


def _block_diag_means(km, n_blk):
    bsz = km.shape[0]
    eye = jnp.eye(N_HEADS, dtype=F32)
    t = km.transpose(0, 2, 3, 1)[:, :, :, None, :] * eye[None, :, None, :, None]
    return _split(t.reshape(bsz, D_MODEL, N_HEADS * n_blk))


HEADS_PER_STEP = 4


def _attn_prompt_kernel(qt_ref, ka_ref, vt_ref, o_ref):
    i = pl.program_id(2)
    blk = MOBA_BLOCK
    rows = lax.broadcasted_iota(jnp.int32, (blk, blk), 0)
    cols = lax.broadcasted_iota(jnp.int32, (blk, blk), 1)
    qts = [qt_ref[hp, 0] for hp in range(HEADS_PER_STEP)]

    def scores(hp, n):
        keys = ka_ref[pl.ds(pl.multiple_of(n * blk, blk), blk), hp * 2 * HEAD_DIM:(hp + 1) * 2 * HEAD_DIM]
        return _dot(keys, qts[hp])

    def absorb(s, vt, m, l, acc):
        m_new = jnp.maximum(m, jnp.max(s, axis=0, keepdims=True))
        alpha = jnp.exp(m - m_new)
        p = jnp.exp(s - m_new)
        l = alpha * l + jnp.sum(p, axis=0, keepdims=True)
        acc = alpha * acc + _dot(vt, p.astype(BF16))
        return m_new, l, acc

    init = tuple(
        (jnp.where(cols >= rows, scores(hp, i), NEG), jnp.full((1, blk), NEG, F32), jnp.zeros((1, blk), F32),
         jnp.zeros((HEAD_DIM, blk), F32))
        for hp in range(HEADS_PER_STEP))

    def body(n, carry):
        prev = jnp.where(n == 0, i, n - 1)
        out = []
        for hp in range(HEADS_PER_STEP):
            s, m, l, acc = carry[hp]
            s_next = scores(hp, n)
            out.append((s_next,) + absorb(s, vt_ref[hp, prev], m, l, acc))
        return tuple(out)

    carry = lax.fori_loop(0, i, body, init)
    last = jnp.where(i == 0, i, i - 1)
    for hp in range(HEADS_PER_STEP):
        s, m, l, acc = carry[hp]
        m, l, acc = absorb(s, vt_ref[hp, last], m, l, acc)
        o_ref[:, hp * HEAD_DIM:(hp + 1) * HEAD_DIM] = (acc / l).T.astype(BF16)


def _attn_prompt(qt, ka, vt, *, bsz, blocks_per_seq):
    n = ka.shape[0]
    seq = blocks_per_seq * MOBA_BLOCK
    hps = HEADS_PER_STEP
    return pl.pallas_call(
        _attn_prompt_kernel,
        out_shape=jax.ShapeDtypeStruct((n, D_MODEL), BF16),
        grid=(bsz, N_HEADS // hps, blocks_per_seq),
        in_specs=[
            pl.BlockSpec((hps, 1, 2 * HEAD_DIM, MOBA_BLOCK), lambda b, h, i: (h, b * blocks_per_seq + i, 0, 0)),
            pl.BlockSpec((seq, hps * 2 * HEAD_DIM), lambda b, h, i: (b, h)),
            pl.BlockSpec((hps, blocks_per_seq, HEAD_DIM, MOBA_BLOCK), lambda b, h, i: (h, b, 0, 0)),
        ],
        out_specs=pl.BlockSpec((MOBA_BLOCK, hps * HEAD_DIM), lambda b, h, i: (b * blocks_per_seq + i, h)),
        compiler_params=_cparams("parallel", "parallel", "arbitrary"),
        name="attn_prompt",
    )(qt, ka, vt)


def _kmean_cache_kernel(pt_ref, *refs):
    page_refs, o_ref = refs[:-1], refs[-1]
    for r in range(len(page_refs) // PAGES_PER_BLOCK):
        tot = jnp.sum(page_refs[2 * r][0], axis=0) + jnp.sum(page_refs[2 * r + 1][0], axis=0)
        o_ref[0, r] = tot / MOBA_BLOCK


def _kmean_cache(cache_k, page_table, *, pages_per_step=8):
    bsz, n_pages = page_table.shape
    steps = n_pages // pages_per_step
    blk_per_step = pages_per_step // PAGES_PER_BLOCK
    pt = page_table.reshape(-1)

    def page_spec(r):
        return pl.BlockSpec((1, PAGE_SIZE, N_HEADS, HEAD_DIM),
                            lambda b, j, pt_ref: (pt_ref[b * n_pages + j * pages_per_step + r], 0, 0, 0))

    out = pl.pallas_call(
        _kmean_cache_kernel,
        out_shape=jax.ShapeDtypeStruct((bsz * steps, blk_per_step, N_HEADS, HEAD_DIM), F32),
        grid_spec=pltpu.PrefetchScalarGridSpec(
            num_scalar_prefetch=1,
            grid=(bsz, steps),
            in_specs=[page_spec(r) for r in range(pages_per_step)],
            out_specs=pl.BlockSpec((1, blk_per_step, N_HEADS, HEAD_DIM), lambda b, j, pt_ref: (b * steps + j, 0, 0, 0)),
        ),
        compiler_params=_cparams("parallel", "arbitrary"),
        name="kmean_cache",
    )(pt, *([cache_k] * pages_per_step))
    return out.reshape(bsz, n_pages // PAGES_PER_BLOCK, N_HEADS, HEAD_DIM)


def _attn_sample_kernel(pt_ref, sel_ref, q_ref, kn_ref, vn_ref, slope_ref, ck_ref, cv_ref, o_ref,
                        kbuf, vbuf, sem, *, n_q, n_pages, past_len):
    b = pl.program_id(0)
    hh = pl.program_id(1)
    step = b * N_HEADS + hh
    n_steps = pl.num_programs(0) * N_HEADS
    cur = step % 2
    sel_base = step * n_q * MOBA_TOP_K

    def copies(step_, buf):
        b_, h_ = step_ // N_HEADS, step_ % N_HEADS
        out = []
        for qi in range(n_q):
            for r in range(MOBA_TOP_K):
                n = sel_ref[step_ * n_q * MOBA_TOP_K + qi * MOBA_TOP_K + r]
                for pg in range(PAGES_PER_BLOCK):
                    page = pt_ref[b_ * n_pages + n * PAGES_PER_BLOCK + pg]
                    dst = pl.ds(pg * PAGE_SIZE, PAGE_SIZE)
                    out.append(pltpu.make_async_copy(ck_ref.at[page, :, h_, :], kbuf.at[buf, qi, r, dst, :], sem.at[buf, 0]))
                    out.append(pltpu.make_async_copy(cv_ref.at[page, :, h_, :], vbuf.at[buf, qi, r, dst, :], sem.at[buf, 1]))
        return out

    @pl.when(step == 0)
    def _():
        for cp in copies(step, cur):
            cp.start()

    @pl.when(step + 1 < n_steps)
    def _():
        for cp in copies(step + 1, 1 - cur):
            cp.start()

    for cp in copies(step, cur):
        cp.wait()

    slope = slope_ref[0][:, 0:1]
    k_new = kn_ref[...]
    v_new = vn_ref[...]
    key_off = lax.broadcasted_iota(jnp.int32, (MOBA_BLOCK, 1), 0)
    new_off = lax.broadcasted_iota(jnp.int32, (SAMPLE_ROWS, 1), 0)
    out_row = lax.broadcasted_iota(jnp.int32, (SAMPLE_ROWS, HEAD_DIM), 0)
    out = jnp.zeros((SAMPLE_ROWS, HEAD_DIM), F32)
    for qi in range(n_q):
        qrow = q_ref[qi:qi + 1, :]
        s_own = jnp.sum(k_new * qrow, axis=-1, keepdims=True) - slope * (qi - new_off).astype(F32)
        s_own = jnp.where(new_off <= qi, s_own, NEG)
        scores = []
        m = jnp.max(s_own, axis=0, keepdims=True)
        for r in range(MOBA_TOP_K):
            n = sel_ref[sel_base + qi * MOBA_TOP_K + r]
            dist = (past_len + qi - n * MOBA_BLOCK - key_off).astype(F32)
            s = jnp.sum(kbuf[cur, qi, r] * qrow, axis=-1, keepdims=True) - slope * dist
            scores.append(s)
            m = jnp.maximum(m, jnp.max(s, axis=0, keepdims=True))
        p_own = jnp.exp(s_own - m)
        l = jnp.sum(p_own, axis=0, keepdims=True)
        acc = jnp.sum(p_own * v_new, axis=0, keepdims=True)
        for r in range(MOBA_TOP_K):
            p = jnp.exp(scores[r] - m)
            l = l + jnp.sum(p, axis=0, keepdims=True)
            acc = acc + jnp.sum(p * vbuf[cur, qi, r], axis=0, keepdims=True)
        out = jnp.where(out_row == qi, acc / l, out)
    o_ref[...] = out.astype(BF16)


def _attn_sample(q, k_new, v_new, sel, cache_k, cache_v, page_table, slopes, *, n_q):
    bsz, n_pages = page_table.shape
    kern = functools.partial(_attn_sample_kernel, n_q=n_q, n_pages=n_pages, past_len=n_pages * PAGE_SIZE)
    tok = pl.BlockSpec((SAMPLE_ROWS, HEAD_DIM), lambda b, h, pt_ref, sel_ref: (b, h))
    return pl.pallas_call(
        kern,
        out_shape=jax.ShapeDtypeStruct(q.shape, BF16),
        grid_spec=pltpu.PrefetchScalarGridSpec(
            num_scalar_prefetch=2,
            grid=(bsz, N_HEADS),
            in_specs=[
                tok, tok, tok,
                pl.BlockSpec((1, 1, LANES), lambda b, h, pt_ref, sel_ref: (h, 0, 0)),
                pl.BlockSpec(memory_space=pl.ANY),
                pl.BlockSpec(memory_space=pl.ANY),
            ],
            out_specs=tok,
            scratch_shapes=[
                pltpu.VMEM((2, n_q, MOBA_TOP_K, MOBA_BLOCK, HEAD_DIM), F32),
                pltpu.VMEM((2, n_q, MOBA_TOP_K, MOBA_BLOCK, HEAD_DIM), F32),
                pltpu.SemaphoreType.DMA((2, 2)),
            ],
        ),
        compiler_params=_cparams("arbitrary", "arbitrary"),
        name="attn_sample",
    )(page_table.reshape(-1), sel, q, k_new, v_new, slopes, cache_k, cache_v)


def _oproj_kernel(h_ref, o_ref, w_ref, out_ref):
    out_ref[...] = h_ref[...] + _dot(o_ref[...], w_ref[...])


def _oproj(h, o, j, w, *, tm):
    n = h.shape[0]
    tok = pl.BlockSpec((tm, D_MODEL), lambda i: (i, 0))
    return pl.pallas_call(
        _oproj_kernel,
        out_shape=jax.ShapeDtypeStruct(h.shape, F32),
        grid=(n // tm,),
        in_specs=[tok, tok, pl.BlockSpec((None, D_MODEL, D_MODEL), lambda i: (j, 0, 0))],
        out_specs=tok,
        compiler_params=_cparams("parallel"),
        name="oproj",
    )(h, o, w["w_o"])


def _prepare_weights(norm_mix, norm_ffn, norm_ple, w_pool, pool_scale, norm_kv, w_kv, k_norm, w_q, q_norm, w_o,
                     router_group_w, router_group_b, router_expert_w, router_expert_b, w_gate, w_up, w_down,
                     w_ple, w_ple_gate):
    pad = LANES - N_GROUPS - N_EXPERTS
    wr = jnp.concatenate([router_group_w, router_expert_w, jnp.zeros((DEPTH, D_MODEL, pad), F32)], axis=-1)
    br = jnp.concatenate([router_group_b, router_expert_b, jnp.zeros((DEPTH, pad), F32)], axis=-1)
    wr_hi, wr_lo = _split(wr)
    wp_hi, wp_lo = _split(w_pool)
    return dict(
        norm_mix=norm_mix.reshape(DEPTH, 1, D_MODEL), norm_ffn=norm_ffn.reshape(DEPTH, 1, D_MODEL),
        norm_ple=norm_ple.reshape(DEPTH, 1, D_MODEL), wp_hi=wp_hi, wp_lo=wp_lo,
        pool_scale=pool_scale.reshape(N_A_LAYERS, 1, D_MODEL), norm_kv=norm_kv.reshape(1, D_MODEL),
        w_kv=w_kv.astype(BF16), k_norm=k_norm.reshape(1, HEAD_DIM), w_q=w_q.astype(BF16),
        q_norm=q_norm.reshape(-1, 1, HEAD_DIM), w_o=w_o.astype(BF16),
        wr_hi=wr_hi, wr_lo=wr_lo, br=br.reshape(DEPTH, 1, LANES),
        w_gate=w_gate.astype(BF16), w_up=w_up.astype(BF16), w_down=w_down.astype(BF16),
        w_ple=w_ple.astype(BF16), w_ple_gate=w_ple_gate.astype(BF16),
    )


def kernel(x_prompt, x_sample, p_prompt, p_sample, state_pool, cache_k, cache_v, page_table, norm_mix, norm_ffn, norm_ple, w_pool, pool_scale, norm_kv, w_kv, k_norm, w_q, q_norm, w_o, router_group_w, router_group_b, router_expert_w, router_expert_b, w_gate, w_up, w_down, w_ple, w_ple_gate):
    w = _prepare_weights(norm_mix, norm_ffn, norm_ple, w_pool, pool_scale, norm_kv, w_kv, k_norm, w_q, q_norm, w_o,
                         router_group_w, router_group_b, router_expert_w, router_expert_b, w_gate, w_up, w_down,
                         w_ple, w_ple_gate)
    slopes = jnp.exp2(-ALIBI_MAX_BIAS * jnp.arange(1, N_HEADS + 1, dtype=F32) / N_HEADS)
    slopes = jnp.broadcast_to(slopes[:, None, None], (N_HEADS, 1, LANES))

    bsz, seq, _ = x_prompt.shape
    dec_b, dec_q, _ = x_sample.shape
    n_pages = page_table.shape[1]
    past_len = n_pages * PAGE_SIZE
    blocks_per_seq = seq // MOBA_BLOCK
    n_past_blk = past_len // MOBA_BLOCK
    n_p, n_s = bsz * seq, dec_b * SAMPLE_ROWS
    row_pad = ((0, 0), (0, 0), (0, SAMPLE_ROWS - dec_q), (0, 0))

    hp = x_prompt
    hs = jnp.pad(x_sample, row_pad[1:])
    pp = p_prompt.reshape(DEPTH, n_p, PLE_DIM)
    ps = jnp.pad(p_sample, row_pad).reshape(DEPTH, n_s, PLE_DIM)
    pre_p = jnp.zeros((bsz, POOL_HALO, D_MODEL), F32)
    pre_s = jnp.pad(state_pool, ((0, 0), (0, 0), (1, 0), (0, 0)))

    pool_p, pool_s = [], []
    kv_p = kv_s = km_p = km_s = None
    for l in range(DEPTH):
        if l < N_A_LAYERS:
            g, sc = w["norm_mix"][l], w["pool_scale"][l]
            hp, st = _pool_mixer(hp.reshape(bsz, seq, D_MODEL), pre_p, g, w["wp_hi"][l], w["wp_lo"][l], sc,
                                 ts=512, n_valid=512, pos0=0)
            pool_p.append(st[:, 1:])
            hs, st = _pool_mixer(hs.reshape(dec_b, SAMPLE_ROWS, D_MODEL), pre_s[l], g, w["wp_hi"][l], w["wp_lo"][l], sc,
                                 ts=SAMPLE_ROWS, n_valid=dec_q, pos0=past_len)
            pool_s.append(st[:, 1:])
            hp, hs = hp.reshape(n_p, D_MODEL), hs.reshape(n_s, D_MODEL)
        else:
            j = l - N_A_LAYERS
            qt = _q_prompt(hp, j, w, *km_p, blocks_per_seq=blocks_per_seq)
            o = _attn_prompt(qt, kv_p[2], kv_p[3], bsz=bsz, blocks_per_seq=blocks_per_seq)
            hp = _oproj(hp, o, j, w, tm=512)
            q, idx = _q_sample(hs, j, w, *km_s, n_blk=n_past_blk)
            sel = idx.reshape(MOBA_TOP_K, dec_b, SAMPLE_ROWS, LANES)[:, :, :dec_q, :N_HEADS]
            sel = sel.transpose(1, 3, 2, 0).reshape(-1)
            o = _attn_sample(q, kv_s[0], kv_s[1], sel, cache_k, cache_v, page_table, slopes, n_q=dec_q)
            hs = _oproj(hs, o, j, w, tm=n_s)
        y = _moe(hp, hs, l, w)
        hp = _ple(hp, y, pp, l, w, tm=512)
        hs = _ple(hs, y, ps, l, w, tm=n_s, y_row0=n_p)
        if l == N_A_LAYERS - 1:
            kv_p = _shared_kv(hp, w, tm=MOBA_BLOCK, blocks_per_seq=blocks_per_seq)
            kv_s = _shared_kv(hs, w, tm=n_s)
            km_p = _block_diag_means(kv_p[4].reshape(bsz, blocks_per_seq, N_HEADS, HEAD_DIM), blocks_per_seq)
            km_s = _block_diag_means(_kmean_cache(cache_k, page_table), n_past_blk)

    heads = (N_HEADS, HEAD_DIM)
    y_prompt = hp.reshape(bsz, seq, D_MODEL)
    y_sample = hs.reshape(dec_b, SAMPLE_ROWS, D_MODEL)[:, :dec_q]
    k_prompt, v_prompt = (t.reshape(bsz, seq, *heads) for t in kv_p[:2])
    k_sample, v_sample = (t.reshape(dec_b, SAMPLE_ROWS, *heads)[:, :dec_q] for t in kv_s[:2])
    return (y_prompt, y_sample, jnp.stack(pool_p), jnp.stack(pool_s), k_prompt, v_prompt, k_sample, v_sample)
```

```python
import functools

import jax
import jax.numpy as jnp
from jax import lax
from jax.experimental import pallas as pl
from jax.experimental.pallas import tpu as pltpu

F32 = jnp.float32
BF16 = jnp.bfloat16

D_MODEL = 1024
DEPTH = 4
N_A_LAYERS = DEPTH // 2
POOL_WINDOWS = (2, 4, 8, 16)
POOL_GROUP_DIM = D_MODEL // len(POOL_WINDOWS)
POOL_STATE = max(POOL_WINDOWS) - 1
POOL_HALO = POOL_STATE + 1
HEAD_DIM = 128
N_HEADS = D_MODEL // HEAD_DIM
MOBA_BLOCK = 256
MOBA_TOP_K = 3
PAGE_SIZE = 128
PAGES_PER_BLOCK = MOBA_BLOCK // PAGE_SIZE
ALIBI_MAX_BIAS = 8.0
N_GROUPS = 4
EXPERTS_PER_GROUP = 4
N_EXPERTS = N_GROUPS * EXPERTS_PER_GROUP
D_EXPERT = D_MODEL // 2
PLE_DIM = 256
NORM_EPS = 1e-6
LANES = 128
SAMPLE_ROWS = 8
NEG = -1e30
VMEM_LIMIT = 48 * 1024 * 1024


def _cparams(*sem):
    return pltpu.CompilerParams(dimension_semantics=sem, vmem_limit_bytes=VMEM_LIMIT)


def _rms(x, g):
    return x * lax.rsqrt(jnp.mean(x * x, axis=-1, keepdims=True) + NORM_EPS) * g


def _dot(a, b):
    return jnp.dot(a, b, preferred_element_type=F32)


def _split(x):
    bits = lax.bitcast_convert_type(x, jnp.uint32) & jnp.uint32(0xFFFF0000)
    hi = lax.bitcast_convert_type(bits, F32)
    return hi.astype(BF16), (x - hi).astype(BF16)


def _dot3(a, b_hi, b_lo):
    a_hi, a_lo = _split(a)
    return _dot(a_hi, b_hi) + (_dot(a_hi, b_lo) + _dot(a_lo, b_hi))


def _sigmoid(x):
    return 1.0 / (1.0 + jnp.exp(-x))


def _pool_kernel(h_ref, pre_ref, g_ref, whi_ref, wlo_ref, sc_ref, o_ref, st_ref, ext_ref, *, ts, n_valid, pos0):
    s = pl.program_id(1)

    @pl.when(s == 0)
    def _():
        ext_ref[0:POOL_HALO, :] = pre_ref[0]

    x = h_ref[0]
    hn = _rms(x, g_ref[...])
    ext_ref[POOL_HALO:POOL_HALO + ts, :] = hn
    pos = pos0 + s * ts + lax.broadcasted_iota(jnp.int32, (ts, 1), 0)
    for g, w in enumerate(POOL_WINDOWS):
        c0, c1 = g * POOL_GROUP_DIM, (g + 1) * POOL_GROUP_DIM
        acc = hn[:, c0:c1]
        for j in range(1, w):
            acc = acc + ext_ref[POOL_HALO - j:POOL_HALO - j + ts, c0:c1]
        cnt = jnp.minimum(pos + 1, w).astype(F32)
        diff = acc / cnt - hn[:, c0:c1]
        y = _dot3(diff, whi_ref[g], wlo_ref[g])
        o_ref[0, :, c0:c1] = x[:, c0:c1] + y * sc_ref[:, c0:c1]
    tail = ext_ref[n_valid:n_valid + POOL_HALO, :]
    st_ref[0] = tail
    ext_ref[0:POOL_HALO, :] = tail


def _pool_mixer(h, prefix, g, w_hi, w_lo, scale, *, ts, n_valid, pos0):
    bsz, seq, _ = h.shape
    kern = functools.partial(_pool_kernel, ts=ts, n_valid=n_valid, pos0=pos0)
    return pl.pallas_call(
        kern,
        out_shape=(jax.ShapeDtypeStruct(h.shape, F32), jax.ShapeDtypeStruct((bsz, POOL_HALO, D_MODEL), F32)),
        grid=(bsz, seq // ts),
        in_specs=[
            pl.BlockSpec((1, ts, D_MODEL), lambda b, s: (b, s, 0)),
            pl.BlockSpec((1, POOL_HALO, D_MODEL), lambda b, s: (b, 0, 0)),
            pl.BlockSpec((1, D_MODEL), lambda b, s: (0, 0)),
            pl.BlockSpec((len(POOL_WINDOWS), POOL_GROUP_DIM, POOL_GROUP_DIM), lambda b, s: (0, 0, 0)),
            pl.BlockSpec((len(POOL_WINDOWS), POOL_GROUP_DIM, POOL_GROUP_DIM), lambda b, s: (0, 0, 0)),
            pl.BlockSpec((1, D_MODEL), lambda b, s: (0, 0)),
        ],
        out_specs=(
            pl.BlockSpec((1, ts, D_MODEL), lambda b, s: (b, s, 0)),
            pl.BlockSpec((1, POOL_HALO, D_MODEL), lambda b, s: (b, 0, 0)),
        ),
        scratch_shapes=[pltpu.VMEM((POOL_HALO + ts, D_MODEL), F32)],
        compiler_params=_cparams("parallel", "arbitrary"),
        name="pool_mixer",
    )(h, prefix, g, w_hi, w_lo, scale)


def _route(xn, wr_hi, wr_lo, br):
    logits = _dot3(xn, wr_hi, wr_lo) + br
    lane = lax.broadcasted_iota(jnp.int32, logits.shape, 1)
    big = jnp.int32(1 << 20)
    lg = jnp.where(lane < N_GROUPS, logits, -jnp.inf)
    mg = jnp.max(lg, axis=-1, keepdims=True)
    g_idx = jnp.min(jnp.where(lg == mg, lane, big), axis=-1, keepdims=True)
    p_grp = 1.0 / jnp.sum(jnp.exp(lg - mg), axis=-1, keepdims=True)
    e0 = N_GROUPS + EXPERTS_PER_GROUP * g_idx
    le = jnp.where((lane >= e0) & (lane < e0 + EXPERTS_PER_GROUP), logits, -jnp.inf)
    m1 = jnp.max(le, axis=-1, keepdims=True)
    i1 = jnp.min(jnp.where(le == m1, lane, big), axis=-1, keepdims=True)
    le2 = jnp.where(lane == i1, -jnp.inf, le)
    m2 = jnp.max(le2, axis=-1, keepdims=True)
    i2 = jnp.min(jnp.where(le2 == m2, lane, big), axis=-1, keepdims=True)
    e2 = jnp.exp(m2 - m1)
    w1 = p_grp / (1.0 + e2)
    w2 = p_grp * e2 / (1.0 + e2)
    return jnp.where(lane == i1, w1, 0.0) + jnp.where(lane == i2, w2, 0.0), g_idx


SLAB = D_MODEL // LANES


def _store_row_major(ref, x):
    for s in range(SLAB):
        ref[pl.ds(s, x.shape[0], stride=SLAB), :] = x[:, s * LANES:(s + 1) * LANES]


def _load_row_major(ref):
    tm = ref.shape[0] // SLAB
    return jnp.concatenate([ref[pl.ds(s, tm, stride=SLAB), :] for s in range(SLAB)], axis=-1)


def _row_major_spec(tm, index_map):
    return pl.BlockSpec((tm * SLAB, LANES), index_map)


def _router_kernel(ha_ref, hb_ref, g_ref, wrh_ref, wrl_ref, br_ref, xn_ref, grp_ref, *, tiles_a):
    h = jnp.where(pl.program_id(0) < tiles_a, ha_ref[...], hb_ref[...])
    xn = _rms(h, g_ref[...])
    _store_row_major(xn_ref, xn)
    _, g_idx = _route(xn, wrh_ref[...], wrl_ref[...], br_ref[...])
    grp_ref[...] = jnp.broadcast_to(g_idx, grp_ref.shape)


def _router(ha, hb, l, w, *, tm):
    tiles_a, tiles_b = ha.shape[0] // tm, hb.shape[0] // tm
    n = ha.shape[0] + hb.shape[0]
    return pl.pallas_call(
        functools.partial(_router_kernel, tiles_a=tiles_a),
        out_shape=(jax.ShapeDtypeStruct((n * SLAB, LANES), F32), jax.ShapeDtypeStruct((n, LANES), jnp.int32)),
        grid=(tiles_a + tiles_b,),
        in_specs=[
            pl.BlockSpec((tm, D_MODEL), lambda i: (jnp.minimum(i, tiles_a - 1), 0)),
            pl.BlockSpec((tm, D_MODEL), lambda i: (jnp.maximum(i - tiles_a, 0), 0)),
            pl.BlockSpec((None, 1, D_MODEL), lambda i: (l, 0, 0)),
            pl.BlockSpec((None, D_MODEL, LANES), lambda i: (l, 0, 0)),
            pl.BlockSpec((None, D_MODEL, LANES), lambda i: (l, 0, 0)),
            pl.BlockSpec((None, 1, LANES), lambda i: (l, 0, 0)),
        ],
        out_specs=(_row_major_spec(tm, lambda i: (i, 0)), pl.BlockSpec((tm, LANES), lambda i: (i, 0))),
        compiler_params=_cparams("parallel"),
        name="router",
    )(ha, hb, w["norm_ffn"], w["wr_hi"], w["wr_lo"], w["br"])


def _slab(r):
    return pl.ds(pl.multiple_of(r * SLAB, SLAB), SLAB)


def _experts_kernel(grp_ref, live_ref, src_ref, dst_ref, x_hbm, wrh_ref, wrl_ref, br_ref, wg_ref, wu_ref, wd_ref,
                    y_hbm, xbuf, obuf, xb_ref, gates_ref, acc_ref, sem_in, sem_out, *, tm, n_tokens):
    i = pl.program_id(0)
    e = pl.program_id(1)
    last_e = EXPERTS_PER_GROUP - 1
    live = live_ref[i] > 0
    prev_live = (i > 0) & (live_ref[jnp.maximum(i - 1, 0)] > 0)
    cur = i % 2
    part = tm // EXPERTS_PER_GROUP

    def fetch(tile, buf, r0, count):
        for r in range(count):
            tok = src_ref[tile * tm + r0 + r]
            pltpu.make_async_copy(x_hbm.at[_slab(tok)], xbuf.at[buf, _slab(r0 + r)], sem_in.at[buf]).start()

    def wait_fetch(buf):
        pltpu.make_async_copy(x_hbm.at[pl.ds(0, tm * SLAB)], xbuf.at[buf], sem_in.at[buf]).wait()

    def emit(tile, r0, count):
        for r in range(count):
            tok = dst_ref[tile * tm + r0 + r]
            pltpu.make_async_copy(obuf.at[_slab(r0 + r)], y_hbm.at[_slab(tok)], sem_out).start()

    def wait_emit():
        pltpu.make_async_copy(obuf, y_hbm.at[pl.ds(0, tm * SLAB)], sem_out).wait()

    @pl.when((i == 0) & (e == 0))
    def _():
        obuf[...] = jnp.zeros_like(obuf)

        def prime(c, carry):
            fetch(0, 0, c * part, part)
            return carry

        lax.fori_loop(0, EXPERTS_PER_GROUP, prime, 0)

    @pl.when((e == 0) & ((i == 0) | prev_live))
    def _():
        wait_fetch(cur)

    @pl.when((e == 0) & live)
    def _():
        x = _load_row_major(xbuf.at[cur])
        xb_ref[...] = x.astype(BF16)
        gates_ref[...] = _route(x, wrh_ref[...], wrl_ref[...], br_ref[...])[0]
        acc_ref[...] = jnp.zeros_like(acc_ref)

    @pl.when(live)
    def _():
        fetch(i + 1, 1 - cur, e * part, part)
        prev = jnp.maximum(i - 1, 0)
        for r in range(part):
            row = e * part + r
            tok = jnp.where(i > 0, dst_ref[prev * tm + row], n_tokens + row)
            pltpu.make_async_copy(obuf.at[_slab(row)], y_hbm.at[_slab(tok)], sem_out).start()
        xb = xb_ref[...]
        a = _dot(xb, wg_ref[...])
        u = _dot(xb, wu_ref[...])
        gates = gates_ref[...]
        lane = lax.broadcasted_iota(jnp.int32, gates.shape, 1)
        expert_lane = N_GROUPS + grp_ref[i] * EXPERTS_PER_GROUP + e
        gate = jnp.sum(jnp.where(lane == expert_lane, gates, 0.0), axis=-1, keepdims=True)
        hid = a * _sigmoid(a) * u * gate
        acc_ref[...] += _dot(hid.astype(BF16), wd_ref[...])

    @pl.when(prev_live & jnp.logical_not(live))
    def _():
        emit(i - 1, e * part, part)

    @pl.when((e == last_e) & (live | prev_live))
    def _():
        wait_emit()

    @pl.when((e == last_e) & live)
    def _():
        _store_row_major(obuf, acc_ref[...])


def _experts(xn, tile_grp, tile_live, src, dst, l, w, *, tm):
    n = xn.shape[0] // SLAB
    n_tiles = tile_grp.shape[0]

    def wspec(shape):
        return pl.BlockSpec((None, None) + shape, lambda i, e, grp, *_: (l, grp[i] * EXPERTS_PER_GROUP + e, 0, 0))

    def const(shape):
        return pl.BlockSpec((None,) + shape, lambda i, e, *_: (l, 0, 0))

    hbm = pl.BlockSpec(memory_space=pl.ANY)
    return pl.pallas_call(
        functools.partial(_experts_kernel, tm=tm, n_tokens=n),
        out_shape=jax.ShapeDtypeStruct(((n + tm) * SLAB, LANES), F32),
        grid_spec=pltpu.PrefetchScalarGridSpec(
            num_scalar_prefetch=4,
            grid=(n_tiles, EXPERTS_PER_GROUP),
            in_specs=[hbm, const((D_MODEL, LANES)), const((D_MODEL, LANES)), const((1, LANES)),
                      wspec((D_MODEL, D_EXPERT)), wspec((D_MODEL, D_EXPERT)), wspec((D_EXPERT, D_MODEL))],
            out_specs=hbm,
            scratch_shapes=[
                pltpu.VMEM((2, tm * SLAB, LANES), F32),
                pltpu.VMEM((tm * SLAB, LANES), F32),
                pltpu.VMEM((tm, D_MODEL), BF16),
                pltpu.VMEM((tm, LANES), F32),
                pltpu.VMEM((tm, D_MODEL), F32),
                pltpu.SemaphoreType.DMA((2,)),
                pltpu.SemaphoreType.DMA(()),
            ],
        ),
        compiler_params=_cparams("arbitrary", "arbitrary"),
        name="experts",
    )(tile_grp, tile_live, src, dst, xn, w["wr_hi"], w["wr_lo"], w["br"], w["w_gate"], w["w_up"], w["w_down"])


ROUTER_TILE = 256
EXPERT_TILES = (640, 512, 384, 256, 128)


def _moe(ha, hb, l, w):
    n = ha.shape[0] + hb.shape[0]
    tm = next(t for t in EXPERT_TILES if n % t == 0)
    xn, grp = _router(ha, hb, l, w, tm=ROUTER_TILE)
    grp = grp[:, 0]
    onehot = (grp[:, None] == jnp.arange(N_GROUPS, dtype=jnp.int32)[None, :]).astype(jnp.int32)
    counts = jnp.sum(onehot, axis=0)
    rank = jnp.sum((jnp.cumsum(onehot, axis=0) - onehot) * onehot, axis=1)
    tiles = (counts + tm - 1) // tm
    tile_end = jnp.cumsum(tiles)
    start = (tile_end - tiles) * tm
    slot = jnp.sum(onehot * start[None, :], axis=1) + rank
    n_tiles = n // tm + N_GROUPS
    tile_id = jnp.arange(n_tiles, dtype=jnp.int32)
    tile_grp = jnp.minimum(jnp.sum((tile_id[:, None] >= tile_end[None, :]).astype(jnp.int32), axis=1), N_GROUPS - 1)
    tile_live = (tile_id < tile_end[-1]).astype(jnp.int32)
    src = jnp.zeros((n_tiles * tm,), jnp.int32).at[slot].set(jnp.arange(n, dtype=jnp.int32))
    pos = jnp.arange(n_tiles * tm, dtype=jnp.int32)
    in_grp = (jnp.repeat(tile_grp, tm)[:, None] == jnp.arange(N_GROUPS, dtype=jnp.int32)[None, :]).astype(jnp.int32)
    filled = pos - jnp.sum(in_grp * start[None, :], axis=1) < jnp.sum(in_grp * counts[None, :], axis=1)
    dst = jnp.where(filled & (jnp.repeat(tile_live, tm) > 0), src, n + pos % tm)
    return _experts(xn, tile_grp, tile_live, src, dst, l, w, tm=tm)


def _ple_kernel(h_ref, y_ref, p_ref, g_ref, wg_ref, wp_ref, o_ref):
    h = h_ref[...] + _load_row_major(y_ref)
    hn = _rms(h, g_ref[...]).astype(BF16)
    gate = _sigmoid(_dot(hn, wg_ref[...]))
    o_ref[...] = h + _dot(p_ref[...].astype(BF16), wp_ref[...]) * gate


def _ple(h, y, p, l, w, *, tm, y_row0=0):
    y_tile0 = y_row0 // tm
    assert y_tile0 * tm == y_row0
    n = h.shape[0]
    return pl.pallas_call(
        _ple_kernel,
        out_shape=jax.ShapeDtypeStruct(h.shape, F32),
        grid=(n // tm,),
        in_specs=[
            pl.BlockSpec((tm, D_MODEL), lambda i: (i, 0)),
            _row_major_spec(tm, lambda i: (i + y_tile0, 0)),
            pl.BlockSpec((None, tm, PLE_DIM), lambda i: (l, i, 0)),
            pl.BlockSpec((None, 1, D_MODEL), lambda i: (l, 0, 0)),
            pl.BlockSpec((None, D_MODEL, D_MODEL), lambda i: (l, 0, 0)),
            pl.BlockSpec((None, PLE_DIM, D_MODEL), lambda i: (l, 0, 0)),
        ],
        out_specs=pl.BlockSpec((tm, D_MODEL), lambda i: (i, 0)),
        compiler_params=_cparams("parallel"),
        name="ple",
    )(h, y, p, w["norm_ple"], w["w_ple_gate"], w["w_ple"])


def _head_rms(x, g):
    outs = []
    for hh in range(N_HEADS):
        outs.append(_rms(x[:, hh * HEAD_DIM:(hh + 1) * HEAD_DIM], g))
    return jnp.concatenate(outs, axis=-1)


FEAT_BLOCK_START = 16
FEAT_OFFSET = 17
FEAT_ONES_ROWS = 8


def _alibi_slope(hh):
    return 2.0 ** (-ALIBI_MAX_BIAS * (hh + 1) / N_HEADS)


assert all(_alibi_slope(hh) == 2.0 ** round(-ALIBI_MAX_BIAS * (hh + 1) / N_HEADS) for hh in range(N_HEADS))


def _kv_kernel(h_ref, g_ref, w_ref, kn_ref, k_ref, v_ref, *attn_refs, blocks_per_seq):
    hn = _rms(h_ref[...], g_ref[...]).astype(BF16)
    kv = _dot(hn, w_ref[...])
    k = _head_rms(kv[:, :D_MODEL], kn_ref[...])
    v = kv[:, D_MODEL:]
    k_ref[...] = k
    v_ref[...] = v
    if not attn_refs:
        return
    ka_ref, vt_ref, km_ref = attn_refs
    km_ref[0] = jnp.sum(k, axis=0, keepdims=True) / MOBA_BLOCK
    tm = k.shape[0]
    blk = pl.program_id(0) % blocks_per_seq
    lane = lax.broadcasted_iota(jnp.int32, (tm, HEAD_DIM), 1)
    off = lax.broadcasted_iota(jnp.int32, (tm, HEAD_DIM), 0).astype(F32)
    onehot = jnp.where(lane == blk, 1.0, 0.0)
    start = (blk * MOBA_BLOCK).astype(F32)
    for hh in range(N_HEADS):
        slope = _alibi_slope(hh)
        feat = onehot + jnp.where(lane == FEAT_BLOCK_START, slope * start, 0.0) \
            + jnp.where(lane == FEAT_OFFSET, slope * off, 0.0)
        c0 = hh * 2 * HEAD_DIM
        ka_ref[:, c0:c0 + HEAD_DIM] = k[:, hh * HEAD_DIM:(hh + 1) * HEAD_DIM].astype(BF16)
        ka_ref[:, c0 + HEAD_DIM:c0 + 2 * HEAD_DIM] = feat.astype(BF16)
        vt_ref[hh, 0] = v[:, hh * HEAD_DIM:(hh + 1) * HEAD_DIM].T.astype(BF16)


def _shared_kv(h, w, *, tm, blocks_per_seq=None):
    n = h.shape[0]
    tok = pl.BlockSpec((tm, D_MODEL), lambda i: (i, 0))
    out_shape = [jax.ShapeDtypeStruct((n, D_MODEL), F32), jax.ShapeDtypeStruct((n, D_MODEL), F32)]
    out_specs = [tok, tok]
    if blocks_per_seq is not None:
        out_shape += [
            jax.ShapeDtypeStruct((n, 2 * D_MODEL), BF16),
            jax.ShapeDtypeStruct((N_HEADS, n // tm, HEAD_DIM, tm), BF16),
            jax.ShapeDtypeStruct((n // tm, 1, D_MODEL), F32),
        ]
        out_specs += [
            pl.BlockSpec((tm, 2 * D_MODEL), lambda i: (i, 0)),
            pl.BlockSpec((N_HEADS, 1, HEAD_DIM, tm), lambda i: (0, i, 0, 0)),
            pl.BlockSpec((1, 1, D_MODEL), lambda i: (i, 0, 0)),
        ]
    return pl.pallas_call(
        functools.partial(_kv_kernel, blocks_per_seq=blocks_per_seq),
        out_shape=tuple(out_shape),
        grid=(n // tm,),
        in_specs=[
            tok,
            pl.BlockSpec((1, D_MODEL), lambda i: (0, 0)),
            pl.BlockSpec((D_MODEL, 2 * D_MODEL), lambda i: (0, 0)),
            pl.BlockSpec((1, HEAD_DIM), lambda i: (0, 0)),
        ],
        out_specs=tuple(out_specs),
        compiler_params=_cparams("parallel"),
        name="shared_kv",
    )(h, w["norm_kv"], w["w_kv"], w["k_norm"])


def _rank_in_segments(gv, seg):
    n_lanes = gv.shape[-1]
    lane = lax.broadcasted_iota(jnp.int32, gv.shape, 1)
    n = lane & (seg - 1)
    rank = jnp.zeros(gv.shape, jnp.int32)
    for k in range(1, seg):
        ahead = pltpu.roll(gv, n_lanes - k, axis=1)
        behind = pltpu.roll(gv, seg - k, axis=1)
        wrap = n + k >= seg
        partner = jnp.where(wrap, behind, ahead)
        rank = rank + jnp.where(wrap, (partner >= gv).astype(jnp.int32), (partner > gv).astype(jnp.int32))
    return rank


def _queries(h_ref, g_ref, wq_ref, qn_ref):
    hn = _rms(h_ref[...], g_ref[...]).astype(BF16)
    return _head_rms(_dot(hn, wq_ref[...]), qn_ref[...])


def _q_prompt_kernel(h_ref, g_ref, wq_ref, qn_ref, kmh_ref, kml_ref, qt_ref, *, blocks_per_seq):
    own = pl.program_id(0) % blocks_per_seq
    q = _queries(h_ref, g_ref, wq_ref, qn_ref)
    tm = q.shape[0]
    gate = _dot3(q, kmh_ref[0], kml_ref[0])
    lane = lax.broadcasted_iota(jnp.int32, gate.shape, 1)
    blk = lane & (blocks_per_seq - 1)
    valid = blk < own
    gv = jnp.where(valid, gate, -jnp.inf)
    rank = _rank_in_segments(gv, blocks_per_seq)
    keep = jnp.where(valid, (rank < MOBA_TOP_K).astype(jnp.int32), (blk == own).astype(jnp.int32))
    bias_t = jnp.where(keep > 0, 0.0, NEG).T
    ones = jnp.ones((FEAT_ONES_ROWS, tm), F32)
    zeros = jnp.zeros((HEAD_DIM - blocks_per_seq - FEAT_ONES_ROWS, tm), F32)
    for hh in range(N_HEADS):
        qh = q[:, hh * HEAD_DIM:(hh + 1) * HEAD_DIM] * HEAD_DIM ** -0.5
        feat = jnp.concatenate([bias_t[hh * blocks_per_seq:(hh + 1) * blocks_per_seq], ones, zeros], axis=0)
        qt_ref[hh, 0, 0:HEAD_DIM, :] = qh.T.astype(BF16)
        qt_ref[hh, 0, HEAD_DIM:2 * HEAD_DIM, :] = feat.astype(BF16)


def _q_prompt(h, j, w, km_hi, km_lo, *, blocks_per_seq):
    assert blocks_per_seq == FEAT_BLOCK_START and N_HEADS * blocks_per_seq == LANES
    n = h.shape[0]
    tm = MOBA_BLOCK
    kern = functools.partial(_q_prompt_kernel, blocks_per_seq=blocks_per_seq)
    return pl.pallas_call(
        kern,
        out_shape=jax.ShapeDtypeStruct((N_HEADS, n // tm, 2 * HEAD_DIM, tm), BF16),
        grid=(n // tm,),
        in_specs=[
            pl.BlockSpec((tm, D_MODEL), lambda i: (i, 0)),
            pl.BlockSpec((None, 1, D_MODEL), lambda i: (N_A_LAYERS + j, 0, 0)),
            pl.BlockSpec((None, D_MODEL, D_MODEL), lambda i: (j, 0, 0)),
            pl.BlockSpec((None, 1, HEAD_DIM), lambda i: (j, 0, 0)),
            pl.BlockSpec((1, D_MODEL, LANES), lambda i: (i // blocks_per_seq, 0, 0)),
            pl.BlockSpec((1, D_MODEL, LANES), lambda i: (i // blocks_per_seq, 0, 0)),
        ],
        out_specs=pl.BlockSpec((N_HEADS, 1, 2 * HEAD_DIM, tm), lambda i: (0, i, 0, 0)),
        compiler_params=_cparams("parallel"),
        name="q_prompt",
    )(h, w["norm_mix"], w["w_q"], w["q_norm"], km_hi, km_lo)


def _q_sample_kernel(h_ref, g_ref, wq_ref, qn_ref, kmh_ref, kml_ref, q_ref, idx_ref, *, n_blk):
    q = _queries(h_ref, g_ref, wq_ref, qn_ref)
    q_ref[...] = q * HEAD_DIM ** -0.5
    gate = _dot3(q, kmh_ref[0], kml_ref[0])
    rank = _rank_in_segments(gate, n_blk)
    lane = lax.broadcasted_iota(jnp.int32, gate.shape, 1)
    blk = (lane & (n_blk - 1)).astype(F32)
    row = lax.broadcasted_iota(jnp.int32, (gate.shape[1], LANES), 0)
    col = lax.broadcasted_iota(jnp.int32, (gate.shape[1], LANES), 1)
    seg_of = jnp.where((row >= col * n_blk) & (row < (col + 1) * n_blk), 1.0, 0.0).astype(BF16)
    for r in range(MOBA_TOP_K):
        picked = jnp.where(rank == r, blk, 0.0).astype(BF16)
        idx_ref[r] = _dot(picked, seg_of).astype(jnp.int32)


def _q_sample(h, j, w, km_hi, km_lo, *, n_blk):
    n = h.shape[0]
    tm = SAMPLE_ROWS
    kern = functools.partial(_q_sample_kernel, n_blk=n_blk)
    return pl.pallas_call(
        kern,
        out_shape=(jax.ShapeDtypeStruct((n, D_MODEL), F32), jax.ShapeDtypeStruct((MOBA_TOP_K, n, LANES), jnp.int32)),
        grid=(n // tm,),
        in_specs=[
            pl.BlockSpec((tm, D_MODEL), lambda i: (i, 0)),
            pl.BlockSpec((None, 1, D_MODEL), lambda i: (N_A_LAYERS + j, 0, 0)),
            pl.BlockSpec((None, D_MODEL, D_MODEL), lambda i: (j, 0, 0)),
            pl.BlockSpec((None, 1, HEAD_DIM), lambda i: (j, 0, 0)),
            pl.BlockSpec((1, D_MODEL, N_HEADS * n_blk), lambda i: (i, 0, 0)),
            pl.BlockSpec((1, D_MODEL, N_HEADS * n_blk), lambda i: (i, 0, 0)),
        ],
        out_specs=(pl.BlockSpec((tm, D_MODEL), lambda i: (i, 0)), pl.BlockSpec((MOBA_TOP_K, tm, LANES), lambda i: (0, i, 0))),
        compiler_params=_cparams("parallel"),
        name="q_sample",
    )(h, w["norm_mix"], w["w_q"], w["q_norm"], km_hi, km_lo)


def _block_diag_means(km, n_blk):
    bsz = km.shape[0]
    eye = jnp.eye(N_HEADS, dtype=F32)
    t = km.transpose(0, 2, 3, 1)[:, :, :, None, :] * eye[None, :, None, :, None]
    return _split(t.reshape(bsz, D_MODEL, N_HEADS * n_blk))


HEADS_PER_STEP = 4


def _attn_prompt_kernel(qt_ref, ka_ref, vt_ref, o_ref):
    i = pl.program_id(2)
    blk = MOBA_BLOCK
    rows = lax.broadcasted_iota(jnp.int32, (blk, blk), 0)
    cols = lax.broadcasted_iota(jnp.int32, (blk, blk), 1)
    qts = [qt_ref[hp, 0] for hp in range(HEADS_PER_STEP)]

    def scores(hp, n):
        keys = ka_ref[pl.ds(pl.multiple_of(n * blk, blk), blk), hp * 2 * HEAD_DIM:(hp + 1) * 2 * HEAD_DIM]
        return _dot(keys, qts[hp])

    def absorb(s, vt, m, l, acc):
        m_new = jnp.maximum(m, jnp.max(s, axis=0, keepdims=True))
        alpha = jnp.exp(m - m_new)
        p = jnp.exp(s - m_new)
        l = alpha * l + jnp.sum(p, axis=0, keepdims=True)
        acc = alpha * acc + _dot(vt, p.astype(BF16))
        return m_new, l, acc

    init = tuple(
        (jnp.where(cols >= rows, scores(hp, i), NEG), jnp.full((1, blk), NEG, F32), jnp.zeros((1, blk), F32),
         jnp.zeros((HEAD_DIM, blk), F32))
        for hp in range(HEADS_PER_STEP))

    def body(n, carry):
        prev = jnp.where(n == 0, i, n - 1)
        out = []
        for hp in range(HEADS_PER_STEP):
            s, m, l, acc = carry[hp]
            s_next = scores(hp, n)
            out.append((s_next,) + absorb(s, vt_ref[hp, prev], m, l, acc))
        return tuple(out)

    carry = lax.fori_loop(0, i, body, init)
    last = jnp.where(i == 0, i, i - 1)
    for hp in range(HEADS_PER_STEP):
        s, m, l, acc = carry[hp]
        m, l, acc = absorb(s, vt_ref[hp, last], m, l, acc)
        o_ref[:, hp * HEAD_DIM:(hp + 1) * HEAD_DIM] = (acc / l).T.astype(BF16)


def _attn_prompt(qt, ka, vt, *, bsz, blocks_per_seq):
    n = ka.shape[0]
    seq = blocks_per_seq * MOBA_BLOCK
    hps = HEADS_PER_STEP
    return pl.pallas_call(
        _attn_prompt_kernel,
        out_shape=jax.ShapeDtypeStruct((n, D_MODEL), BF16),
        grid=(bsz, N_HEADS // hps, blocks_per_seq),
        in_specs=[
            pl.BlockSpec((hps, 1, 2 * HEAD_DIM, MOBA_BLOCK), lambda b, h, i: (h, b * blocks_per_seq + i, 0, 0)),
            pl.BlockSpec((seq, hps * 2 * HEAD_DIM), lambda b, h, i: (b, h)),
            pl.BlockSpec((hps, blocks_per_seq, HEAD_DIM, MOBA_BLOCK), lambda b, h, i: (h, b, 0, 0)),
        ],
        out_specs=pl.BlockSpec((MOBA_BLOCK, hps * HEAD_DIM), lambda b, h, i: (b * blocks_per_seq + i, h)),
        compiler_params=_cparams("parallel", "parallel", "arbitrary"),
        name="attn_prompt",
    )(qt, ka, vt)


def _kmean_cache_kernel(pt_ref, *refs):
    page_refs, o_ref = refs[:-1], refs[-1]
    for r in range(len(page_refs) // PAGES_PER_BLOCK):
        tot = jnp.sum(page_refs[2 * r][0], axis=0) + jnp.sum(page_refs[2 * r + 1][0], axis=0)
        o_ref[0, r] = tot / MOBA_BLOCK


def _kmean_cache(cache_k, page_table, *, pages_per_step=8):
    bsz, n_pages = page_table.shape
    steps = n_pages // pages_per_step
    blk_per_step = pages_per_step // PAGES_PER_BLOCK
    pt = page_table.reshape(-1)

    def page_spec(r):
        return pl.BlockSpec((1, PAGE_SIZE, N_HEADS, HEAD_DIM),
                            lambda b, j, pt_ref: (pt_ref[b * n_pages + j * pages_per_step + r], 0, 0, 0))

    out = pl.pallas_call(
        _kmean_cache_kernel,
        out_shape=jax.ShapeDtypeStruct((bsz * steps, blk_per_step, N_HEADS, HEAD_DIM), F32),
        grid_spec=pltpu.PrefetchScalarGridSpec(
            num_scalar_prefetch=1,
            grid=(bsz, steps),
            in_specs=[page_spec(r) for r in range(pages_per_step)],
            out_specs=pl.BlockSpec((1, blk_per_step, N_HEADS, HEAD_DIM), lambda b, j, pt_ref: (b * steps + j, 0, 0, 0)),
        ),
        compiler_params=_cparams("parallel", "arbitrary"),
        name="kmean_cache",
    )(pt, *([cache_k] * pages_per_step))
    return out.reshape(bsz, n_pages // PAGES_PER_BLOCK, N_HEADS, HEAD_DIM)


def _attn_sample_kernel(pt_ref, sel_ref, q_ref, kn_ref, vn_ref, slope_ref, ck_ref, cv_ref, o_ref,
                        kbuf, vbuf, sem, *, n_q, n_pages, past_len):
    b = pl.program_id(0)
    hh = pl.program_id(1)
    step = b * N_HEADS + hh
    n_steps = pl.num_programs(0) * N_HEADS
    cur = step % 2
    sel_base = step * n_q * MOBA_TOP_K

    def copies(step_, buf):
        b_, h_ = step_ // N_HEADS, step_ % N_HEADS
        out = []
        for qi in range(n_q):
            for r in range(MOBA_TOP_K):
                n = sel_ref[step_ * n_q * MOBA_TOP_K + qi * MOBA_TOP_K + r]
                for pg in range(PAGES_PER_BLOCK):
                    page = pt_ref[b_ * n_pages + n * PAGES_PER_BLOCK + pg]
                    dst = pl.ds(pg * PAGE_SIZE, PAGE_SIZE)
                    out.append(pltpu.make_async_copy(ck_ref.at[page, :, h_, :], kbuf.at[buf, qi, r, dst, :], sem.at[buf, 0]))
                    out.append(pltpu.make_async_copy(cv_ref.at[page, :, h_, :], vbuf.at[buf, qi, r, dst, :], sem.at[buf, 1]))
        return out

    @pl.when(step == 0)
    def _():
        for cp in copies(step, cur):
            cp.start()

    @pl.when(step + 1 < n_steps)
    def _():
        for cp in copies(step + 1, 1 - cur):
            cp.start()

    for cp in copies(step, cur):
        cp.wait()

    slope = slope_ref[0][:, 0:1]
    k_new = kn_ref[...]
    v_new = vn_ref[...]
    key_off = lax.broadcasted_iota(jnp.int32, (MOBA_BLOCK, 1), 0)
    new_off = lax.broadcasted_iota(jnp.int32, (SAMPLE_ROWS, 1), 0)
    out_row = lax.broadcasted_iota(jnp.int32, (SAMPLE_ROWS, HEAD_DIM), 0)
    out = jnp.zeros((SAMPLE_ROWS, HEAD_DIM), F32)
    for qi in range(n_q):
        qrow = q_ref[qi:qi + 1, :]
        s_own = jnp.sum(k_new * qrow, axis=-1, keepdims=True) - slope * (qi - new_off).astype(F32)
        s_own = jnp.where(new_off <= qi, s_own, NEG)
        scores = []
        m = jnp.max(s_own, axis=0, keepdims=True)
        for r in range(MOBA_TOP_K):
            n = sel_ref[sel_base + qi * MOBA_TOP_K + r]
            dist = (past_len + qi - n * MOBA_BLOCK - key_off).astype(F32)
            s = jnp.sum(kbuf[cur, qi, r] * qrow, axis=-1, keepdims=True) - slope * dist
            scores.append(s)
            m = jnp.maximum(m, jnp.max(s, axis=0, keepdims=True))
        p_own = jnp.exp(s_own - m)
        l = jnp.sum(p_own, axis=0, keepdims=True)
        acc = jnp.sum(p_own * v_new, axis=0, keepdims=True)
        for r in range(MOBA_TOP_K):
            p = jnp.exp(scores[r] - m)
            l = l + jnp.sum(p, axis=0, keepdims=True)
            acc = acc + jnp.sum(p * vbuf[cur, qi, r], axis=0, keepdims=True)
        out = jnp.where(out_row == qi, acc / l, out)
    o_ref[...] = out.astype(BF16)


def _attn_sample(q, k_new, v_new, sel, cache_k, cache_v, page_table, slopes, *, n_q):
    bsz, n_pages = page_table.shape
    kern = functools.partial(_attn_sample_kernel, n_q=n_q, n_pages=n_pages, past_len=n_pages * PAGE_SIZE)
    tok = pl.BlockSpec((SAMPLE_ROWS, HEAD_DIM), lambda b, h, pt_ref, sel_ref: (b, h))
    return pl.pallas_call(
        kern,
        out_shape=jax.ShapeDtypeStruct(q.shape, BF16),
        grid_spec=pltpu.PrefetchScalarGridSpec(
            num_scalar_prefetch=2,
            grid=(bsz, N_HEADS),
            in_specs=[
                tok, tok, tok,
                pl.BlockSpec((1, 1, LANES), lambda b, h, pt_ref, sel_ref: (h, 0, 0)),
                pl.BlockSpec(memory_space=pl.ANY),
                pl.BlockSpec(memory_space=pl.ANY),
            ],
            out_specs=tok,
            scratch_shapes=[
                pltpu.VMEM((2, n_q, MOBA_TOP_K, MOBA_BLOCK, HEAD_DIM), F32),
                pltpu.VMEM((2, n_q, MOBA_TOP_K, MOBA_BLOCK, HEAD_DIM), F32),
                pltpu.SemaphoreType.DMA((2, 2)),
            ],
        ),
        compiler_params=_cparams("arbitrary", "arbitrary"),
        name="attn_sample",
    )(page_table.reshape(-1), sel, q, k_new, v_new, slopes, cache_k, cache_v)


def _oproj_kernel(h_ref, o_ref, w_ref, out_ref):
    out_ref[...] = h_ref[...] + _dot(o_ref[...], w_ref[...])


def _oproj(h, o, j, w, *, tm):
    n = h.shape[0]
    tok = pl.BlockSpec((tm, D_MODEL), lambda i: (i, 0))
    return pl.pallas_call(
        _oproj_kernel,
        out_shape=jax.ShapeDtypeStruct(h.shape, F32),
        grid=(n // tm,),
        in_specs=[tok, tok, pl.BlockSpec((None, D_MODEL, D_MODEL), lambda i: (j, 0, 0))],
        out_specs=tok,
        compiler_params=_cparams("parallel"),
        name="oproj",
    )(h, o, w["w_o"])


def _prepare_weights(norm_mix, norm_ffn, norm_ple, w_pool, pool_scale, norm_kv, w_kv, k_norm, w_q, q_norm, w_o,
                     router_group_w, router_group_b, router_expert_w, router_expert_b, w_gate, w_up, w_down,
                     w_ple, w_ple_gate):
    pad = LANES - N_GROUPS - N_EXPERTS
    wr = jnp.concatenate([router_group_w, router_expert_w, jnp.zeros((DEPTH, D_MODEL, pad), F32)], axis=-1)
    br = jnp.concatenate([router_group_b, router_expert_b, jnp.zeros((DEPTH, pad), F32)], axis=-1)
    wr_hi, wr_lo = _split(wr)
    wp_hi, wp_lo = _split(w_pool)
    return dict(
        norm_mix=norm_mix.reshape(DEPTH, 1, D_MODEL), norm_ffn=norm_ffn.reshape(DEPTH, 1, D_MODEL),
        norm_ple=norm_ple.reshape(DEPTH, 1, D_MODEL), wp_hi=wp_hi, wp_lo=wp_lo,
        pool_scale=pool_scale.reshape(N_A_LAYERS, 1, D_MODEL), norm_kv=norm_kv.reshape(1, D_MODEL),
        w_kv=w_kv.astype(BF16), k_norm=k_norm.reshape(1, HEAD_DIM), w_q=w_q.astype(BF16),
        q_norm=q_norm.reshape(-1, 1, HEAD_DIM), w_o=w_o.astype(BF16),
        wr_hi=wr_hi, wr_lo=wr_lo, br=br.reshape(DEPTH, 1, LANES),
        w_gate=w_gate.astype(BF16), w_up=w_up.astype(BF16), w_down=w_down.astype(BF16),
        w_ple=w_ple.astype(BF16), w_ple_gate=w_ple_gate.astype(BF16),
    )


def kernel(x_prompt, x_sample, p_prompt, p_sample, state_pool, cache_k, cache_v, page_table, norm_mix, norm_ffn, norm_ple, w_pool, pool_scale, norm_kv, w_kv, k_norm, w_q, q_norm, w_o, router_group_w, router_group_b, router_expert_w, router_expert_b, w_gate, w_up, w_down, w_ple, w_ple_gate):
    w = _prepare_weights(norm_mix, norm_ffn, norm_ple, w_pool, pool_scale, norm_kv, w_kv, k_norm, w_q, q_norm, w_o,
                         router_group_w, router_group_b, router_expert_w, router_expert_b, w_gate, w_up, w_down,
                         w_ple, w_ple_gate)
    slopes = jnp.exp2(-ALIBI_MAX_BIAS * jnp.arange(1, N_HEADS + 1, dtype=F32) / N_HEADS)
    slopes = jnp.broadcast_to(slopes[:, None, None], (N_HEADS, 1, LANES))

    bsz, seq, _ = x_prompt.shape
    dec_b, dec_q, _ = x_sample.shape
    n_pages = page_table.shape[1]
    past_len = n_pages * PAGE_SIZE
    blocks_per_seq = seq // MOBA_BLOCK
    n_past_blk = past_len // MOBA_BLOCK
    n_p, n_s = bsz * seq, dec_b * SAMPLE_ROWS
    row_pad = ((0, 0), (0, 0), (0, SAMPLE_ROWS - dec_q), (0, 0))

    hp = x_prompt
    hs = jnp.pad(x_sample, row_pad[1:])
    pp = p_prompt.reshape(DEPTH, n_p, PLE_DIM)
    ps = jnp.pad(p_sample, row_pad).reshape(DEPTH, n_s, PLE_DIM)
    pre_p = jnp.zeros((bsz, POOL_HALO, D_MODEL), F32)
    pre_s = jnp.pad(state_pool, ((0, 0), (0, 0), (1, 0), (0, 0)))

    pool_p, pool_s = [], []
    kv_p = kv_s = km_p = km_s = None
    for l in range(DEPTH):
        if l < N_A_LAYERS:
            g, sc = w["norm_mix"][l], w["pool_scale"][l]
            hp, st = _pool_mixer(hp.reshape(bsz, seq, D_MODEL), pre_p, g, w["wp_hi"][l], w["wp_lo"][l], sc,
                                 ts=512, n_valid=512, pos0=0)
            pool_p.append(st[:, 1:])
            hs, st = _pool_mixer(hs.reshape(dec_b, SAMPLE_ROWS, D_MODEL), pre_s[l], g, w["wp_hi"][l], w["wp_lo"][l], sc,
                                 ts=SAMPLE_ROWS, n_valid=dec_q, pos0=past_len)
            pool_s.append(st[:, 1:])
            hp, hs = hp.reshape(n_p, D_MODEL), hs.reshape(n_s, D_MODEL)
        else:
            j = l - N_A_LAYERS
            qt = _q_prompt(hp, j, w, *km_p, blocks_per_seq=blocks_per_seq)
            o = _attn_prompt(qt, kv_p[2], kv_p[3], bsz=bsz, blocks_per_seq=blocks_per_seq)
            hp = _oproj(hp, o, j, w, tm=512)
            q, idx = _q_sample(hs, j, w, *km_s, n_blk=n_past_blk)
            sel = idx.reshape(MOBA_TOP_K, dec_b, SAMPLE_ROWS, LANES)[:, :, :dec_q, :N_HEADS]
            sel = sel.transpose(1, 3, 2, 0).reshape(-1)
            o = _attn_sample(q, kv_s[0], kv_s[1], sel, cache_k, cache_v, page_table, slopes, n_q=dec_q)
            hs = _oproj(hs, o, j, w, tm=n_s)
        y = _moe(hp, hs, l, w)
        hp = _ple(hp, y, pp, l, w, tm=512)
        hs = _ple(hs, y, ps, l, w, tm=n_s, y_row0=n_p)
        if l == N_A_LAYERS - 1:
            kv_p = _shared_kv(hp, w, tm=MOBA_BLOCK, blocks_per_seq=blocks_per_seq)
            kv_s = _shared_kv(hs, w, tm=n_s)
            km_p = _block_diag_means(kv_p[4].reshape(bsz, blocks_per_seq, N_HEADS, HEAD_DIM), blocks_per_seq)
            km_s = _block_diag_means(_kmean_cache(cache_k, page_table), n_past_blk)

    heads = (N_HEADS, HEAD_DIM)
    y_prompt = hp.reshape(bsz, seq, D_MODEL)
    y_sample = hs.reshape(dec_b, SAMPLE_ROWS, D_MODEL)[:, :dec_q]
    k_prompt, v_prompt = (t.reshape(bsz, seq, *heads) for t in kv_p[:2])
    k_sample, v_sample = (t.reshape(dec_b, SAMPLE_ROWS, *heads)[:, :dec_q] for t in kv_s[:2])
    return (y_prompt, y_sample, jnp.stack(pool_p), jnp.stack(pool_s), k_prompt, v_prompt, k_sample, v_sample)
```

```python
import functools

import jax
import jax.numpy as jnp
from jax import lax
from jax.experimental import pallas as pl
from jax.experimental.pallas import tpu as pltpu

F32 = jnp.float32
BF16 = jnp.bfloat16

D_MODEL = 1024
DEPTH = 4
N_A_LAYERS = DEPTH // 2
POOL_WINDOWS = (2, 4, 8, 16)
POOL_GROUP_DIM = D_MODEL // len(POOL_WINDOWS)
POOL_STATE = max(POOL_WINDOWS) - 1
POOL_HALO = POOL_STATE + 1
HEAD_DIM = 128
N_HEADS = D_MODEL // HEAD_DIM
MOBA_BLOCK = 256
MOBA_TOP_K = 3
PAGE_SIZE = 128
PAGES_PER_BLOCK = MOBA_BLOCK // PAGE_SIZE
ALIBI_MAX_BIAS = 8.0
N_GROUPS = 4
EXPERTS_PER_GROUP = 4
N_EXPERTS = N_GROUPS * EXPERTS_PER_GROUP
D_EXPERT = D_MODEL // 2
PLE_DIM = 256
NORM_EPS = 1e-6
LANES = 128
SAMPLE_ROWS = 8
NEG = -1e30
VMEM_LIMIT = 48 * 1024 * 1024


def _cparams(*sem):
    return pltpu.CompilerParams(dimension_semantics=sem, vmem_limit_bytes=VMEM_LIMIT)


def _rms(x, g):
    return x * lax.rsqrt(jnp.mean(x * x, axis=-1, keepdims=True) + NORM_EPS) * g


def _dot(a, b):
    return jnp.dot(a, b, preferred_element_type=F32)


def _split(x):
    bits = lax.bitcast_convert_type(x, jnp.uint32) & jnp.uint32(0xFFFF0000)
    hi = lax.bitcast_convert_type(bits, F32)
    return hi.astype(BF16), (x - hi).astype(BF16)


def _dot3(a, b_hi, b_lo):
    a_hi, a_lo = _split(a)
    return _dot(a_hi, b_hi) + (_dot(a_hi, b_lo) + _dot(a_lo, b_hi))


def _sigmoid(x):
    return 1.0 / (1.0 + jnp.exp(-x))


def _pool_kernel(h_ref, pre_ref, g_ref, whi_ref, wlo_ref, sc_ref, o_ref, st_ref, ext_ref, *, ts, n_valid, pos0):
    s = pl.program_id(1)

    @pl.when(s == 0)
    def _():
        ext_ref[0:POOL_HALO, :] = pre_ref[0]

    x = h_ref[0]
    hn = _rms(x, g_ref[...])
    ext_ref[POOL_HALO:POOL_HALO + ts, :] = hn
    pos = pos0 + s * ts + lax.broadcasted_iota(jnp.int32, (ts, 1), 0)
    for g, w in enumerate(POOL_WINDOWS):
        c0, c1 = g * POOL_GROUP_DIM, (g + 1) * POOL_GROUP_DIM
        acc = hn[:, c0:c1]
        for j in range(1, w):
            acc = acc + ext_ref[POOL_HALO - j:POOL_HALO - j + ts, c0:c1]
        cnt = jnp.minimum(pos + 1, w).astype(F32)
        diff = acc / cnt - hn[:, c0:c1]
        y = _dot3(diff, whi_ref[g], wlo_ref[g])
        o_ref[0, :, c0:c1] = x[:, c0:c1] + y * sc_ref[:, c0:c1]
    tail = ext_ref[n_valid:n_valid + POOL_HALO, :]
    st_ref[0] = tail
    ext_ref[0:POOL_HALO, :] = tail


def _pool_mixer(h, prefix, g, w_hi, w_lo, scale, *, ts, n_valid, pos0):
    bsz, seq, _ = h.shape
    kern = functools.partial(_pool_kernel, ts=ts, n_valid=n_valid, pos0=pos0)
    return pl.pallas_call(
        kern,
        out_shape=(jax.ShapeDtypeStruct(h.shape, F32), jax.ShapeDtypeStruct((bsz, POOL_HALO, D_MODEL), F32)),
        grid=(bsz, seq // ts),
        in_specs=[
            pl.BlockSpec((1, ts, D_MODEL), lambda b, s: (b, s, 0)),
            pl.BlockSpec((1, POOL_HALO, D_MODEL), lambda b, s: (b, 0, 0)),
            pl.BlockSpec((1, D_MODEL), lambda b, s: (0, 0)),
            pl.BlockSpec((len(POOL_WINDOWS), POOL_GROUP_DIM, POOL_GROUP_DIM), lambda b, s: (0, 0, 0)),
            pl.BlockSpec((len(POOL_WINDOWS), POOL_GROUP_DIM, POOL_GROUP_DIM), lambda b, s: (0, 0, 0)),
            pl.BlockSpec((1, D_MODEL), lambda b, s: (0, 0)),
        ],
        out_specs=(
            pl.BlockSpec((1, ts, D_MODEL), lambda b, s: (b, s, 0)),
            pl.BlockSpec((1, POOL_HALO, D_MODEL), lambda b, s: (b, 0, 0)),
        ),
        scratch_shapes=[pltpu.VMEM((POOL_HALO + ts, D_MODEL), F32)],
        compiler_params=_cparams("parallel", "arbitrary"),
        name="pool_mixer",
    )(h, prefix, g, w_hi, w_lo, scale)


def _route(xn, wr_hi, wr_lo, br):
    logits = _dot3(xn, wr_hi, wr_lo) + br
    lane = lax.broadcasted_iota(jnp.int32, logits.shape, 1)
    big = jnp.int32(1 << 20)
    lg = jnp.where(lane < N_GROUPS, logits, -jnp.inf)
    mg = jnp.max(lg, axis=-1, keepdims=True)
    g_idx = jnp.min(jnp.where(lg == mg, lane, big), axis=-1, keepdims=True)
    p_grp = 1.0 / jnp.sum(jnp.exp(lg - mg), axis=-1, keepdims=True)
    e0 = N_GROUPS + EXPERTS_PER_GROUP * g_idx
    le = jnp.where((lane >= e0) & (lane < e0 + EXPERTS_PER_GROUP), logits, -jnp.inf)
    m1 = jnp.max(le, axis=-1, keepdims=True)
    i1 = jnp.min(jnp.where(le == m1, lane, big), axis=-1, keepdims=True)
    le2 = jnp.where(lane == i1, -jnp.inf, le)
    m2 = jnp.max(le2, axis=-1, keepdims=True)
    i2 = jnp.min(jnp.where(le2 == m2, lane, big), axis=-1, keepdims=True)
    e2 = jnp.exp(m2 - m1)
    w1 = p_grp / (1.0 + e2)
    w2 = p_grp * e2 / (1.0 + e2)
    return jnp.where(lane == i1, w1, 0.0) + jnp.where(lane == i2, w2, 0.0), g_idx


SLAB = D_MODEL // LANES


def _store_row_major(ref, x):
    for s in range(SLAB):
        ref[pl.ds(s, x.shape[0], stride=SLAB), :] = x[:, s * LANES:(s + 1) * LANES]


def _load_row_major(ref):
    tm = ref.shape[0] // SLAB
    return jnp.concatenate([ref[pl.ds(s, tm, stride=SLAB), :] for s in range(SLAB)], axis=-1)


def _row_major_spec(tm, index_map):
    return pl.BlockSpec((tm * SLAB, LANES), index_map)


def _router_kernel(ha_ref, hb_ref, g_ref, wrh_ref, wrl_ref, br_ref, xn_ref, grp_ref, *, tiles_a):
    h = jnp.where(pl.program_id(0) < tiles_a, ha_ref[...], hb_ref[...])
    xn = _rms(h, g_ref[...])
    _store_row_major(xn_ref, xn)
    _, g_idx = _route(xn, wrh_ref[...], wrl_ref[...], br_ref[...])
    grp_ref[...] = jnp.broadcast_to(g_idx, grp_ref.shape)


def _router(ha, hb, l, w, *, tm):
    tiles_a, tiles_b = ha.shape[0] // tm, hb.shape[0] // tm
    n = ha.shape[0] + hb.shape[0]
    return pl.pallas_call(
        functools.partial(_router_kernel, tiles_a=tiles_a),
        out_shape=(jax.ShapeDtypeStruct((n * SLAB, LANES), F32), jax.ShapeDtypeStruct((n, LANES), jnp.int32)),
        grid=(tiles_a + tiles_b,),
        in_specs=[
            pl.BlockSpec((tm, D_MODEL), lambda i: (jnp.minimum(i, tiles_a - 1), 0)),
            pl.BlockSpec((tm, D_MODEL), lambda i: (jnp.maximum(i - tiles_a, 0), 0)),
            pl.BlockSpec((None, 1, D_MODEL), lambda i: (l, 0, 0)),
            pl.BlockSpec((None, D_MODEL, LANES), lambda i: (l, 0, 0)),
            pl.BlockSpec((None, D_MODEL, LANES), lambda i: (l, 0, 0)),
            pl.BlockSpec((None, 1, LANES), lambda i: (l, 0, 0)),
        ],
        out_specs=(_row_major_spec(tm, lambda i: (i, 0)), pl.BlockSpec((tm, LANES), lambda i: (i, 0))),
        compiler_params=_cparams("parallel"),
        name="router",
    )(ha, hb, w["norm_ffn"], w["wr_hi"], w["wr_lo"], w["br"])


def _slab(r):
    return pl.ds(pl.multiple_of(r * SLAB, SLAB), SLAB)


def _experts_kernel(grp_ref, live_ref, src_ref, dst_ref, x_hbm, wrh_ref, wrl_ref, br_ref, wg_ref, wu_ref, wd_ref,
                    y_hbm, xbuf, obuf, xb_ref, gates_ref, acc_ref, sem_in, sem_out, *, tm, n_tokens):
    i = pl.program_id(0)
    e = pl.program_id(1)
    last_e = EXPERTS_PER_GROUP - 1
    live = live_ref[i] > 0
    prev_live = (i > 0) & (live_ref[jnp.maximum(i - 1, 0)] > 0)
    cur = i % 2
    part = tm // EXPERTS_PER_GROUP

    def fetch(tile, buf, r0, count):
        for r in range(count):
            tok = src_ref[tile * tm + r0 + r]
            pltpu.make_async_copy(x_hbm.at[_slab(tok)], xbuf.at[buf, _slab(r0 + r)], sem_in.at[buf]).start()

    def wait_fetch(buf):
        pltpu.make_async_copy(x_hbm.at[pl.ds(0, tm * SLAB)], xbuf.at[buf], sem_in.at[buf]).wait()

    def emit(tile, r0, count):
        for r in range(count):
            tok = dst_ref[tile * tm + r0 + r]
            pltpu.make_async_copy(obuf.at[_slab(r0 + r)], y_hbm.at[_slab(tok)], sem_out).start()

    def wait_emit():
        pltpu.make_async_copy(obuf, y_hbm.at[pl.ds(0, tm * SLAB)], sem_out).wait()

    @pl.when((i == 0) & (e == 0))
    def _():
        obuf[...] = jnp.zeros_like(obuf)

        def prime(c, carry):
            fetch(0, 0, c * part, part)
            return carry

        lax.fori_loop(0, EXPERTS_PER_GROUP, prime, 0)

    @pl.when((e == 0) & ((i == 0) | prev_live))
    def _():
        wait_fetch(cur)

    @pl.when((e == 0) & live)
    def _():
        x = _load_row_major(xbuf.at[cur])
        xb_ref[...] = x.astype(BF16)
        gates_ref[...] = _route(x, wrh_ref[...], wrl_ref[...], br_ref[...])[0]
        acc_ref[...] = jnp.zeros_like(acc_ref)

    @pl.when(live)
    def _():
        fetch(i + 1, 1 - cur, e * part, part)
        prev = jnp.maximum(i - 1, 0)
        for r in range(part):
            row = e * part + r
            tok = jnp.where(i > 0, dst_ref[prev * tm + row], n_tokens + row)
            pltpu.make_async_copy(obuf.at[_slab(row)], y_hbm.at[_slab(tok)], sem_out).start()
        xb = xb_ref[...]
        a = _dot(xb, wg_ref[...])
        u = _dot(xb, wu_ref[...])
        gates = gates_ref[...]
        lane = lax.broadcasted_iota(jnp.int32, gates.shape, 1)
        expert_lane = N_GROUPS + grp_ref[i] * EXPERTS_PER_GROUP + e
        gate = jnp.sum(jnp.where(lane == expert_lane, gates, 0.0), axis=-1, keepdims=True)
        hid = a * _sigmoid(a) * u * gate
        acc_ref[...] += _dot(hid.astype(BF16), wd_ref[...])

    @pl.when(prev_live & jnp.logical_not(live))
    def _():
        emit(i - 1, e * part, part)

    @pl.when((e == last_e) & (live | prev_live))
    def _():
        wait_emit()

    @pl.when((e == last_e) & live)
    def _():
        _store_row_major(obuf, acc_ref[...])


def _experts(xn, tile_grp, tile_live, src, dst, l, w, *, tm):
    n = xn.shape[0] // SLAB
    n_tiles = tile_grp.shape[0]

    def wspec(shape):
        return pl.BlockSpec((None, None) + shape, lambda i, e, grp, *_: (l, grp[i] * EXPERTS_PER_GROUP + e, 0, 0))

    def const(shape):
        return pl.BlockSpec((None,) + shape, lambda i, e, *_: (l, 0, 0))

    hbm = pl.BlockSpec(memory_space=pl.ANY)
    return pl.pallas_call(
        functools.partial(_experts_kernel, tm=tm, n_tokens=n),
        out_shape=jax.ShapeDtypeStruct(((n + tm) * SLAB, LANES), F32),
        grid_spec=pltpu.PrefetchScalarGridSpec(
            num_scalar_prefetch=4,
            grid=(n_tiles, EXPERTS_PER_GROUP),
            in_specs=[hbm, const((D_MODEL, LANES)), const((D_MODEL, LANES)), const((1, LANES)),
                      wspec((D_MODEL, D_EXPERT)), wspec((D_MODEL, D_EXPERT)), wspec((D_EXPERT, D_MODEL))],
            out_specs=hbm,
            scratch_shapes=[
                pltpu.VMEM((2, tm * SLAB, LANES), F32),
                pltpu.VMEM((tm * SLAB, LANES), F32),
                pltpu.VMEM((tm, D_MODEL), BF16),
                pltpu.VMEM((tm, LANES), F32),
                pltpu.VMEM((tm, D_MODEL), F32),
                pltpu.SemaphoreType.DMA((2,)),
                pltpu.SemaphoreType.DMA(()),
            ],
        ),
        compiler_params=_cparams("arbitrary", "arbitrary"),
        name="experts",
    )(tile_grp, tile_live, src, dst, xn, w["wr_hi"], w["wr_lo"], w["br"], w["w_gate"], w["w_up"], w["w_down"])


ROUTER_TILE = 256
EXPERT_TILES = (640, 512, 384, 256, 128)


def _moe(ha, hb, l, w):
    n = ha.shape[0] + hb.shape[0]
    tm = next(t for t in EXPERT_TILES if n % t == 0)
    xn, grp = _router(ha, hb, l, w, tm=ROUTER_TILE)
    grp = grp[:, 0]
    onehot = (grp[:, None] == jnp.arange(N_GROUPS, dtype=jnp.int32)[None, :]).astype(jnp.int32)
    counts = jnp.sum(onehot, axis=0)
    rank = jnp.sum((jnp.cumsum(onehot, axis=0) - onehot) * onehot, axis=1)
    tiles = (counts + tm - 1) // tm
    tile_end = jnp.cumsum(tiles)
    start = (tile_end - tiles) * tm
    slot = jnp.sum(onehot * start[None, :], axis=1) + rank
    n_tiles = n // tm + N_GROUPS
    tile_id = jnp.arange(n_tiles, dtype=jnp.int32)
    tile_grp = jnp.minimum(jnp.sum((tile_id[:, None] >= tile_end[None, :]).astype(jnp.int32), axis=1), N_GROUPS - 1)
    tile_live = (tile_id < tile_end[-1]).astype(jnp.int32)
    src = jnp.zeros((n_tiles * tm,), jnp.int32).at[slot].set(jnp.arange(n, dtype=jnp.int32))
    pos = jnp.arange(n_tiles * tm, dtype=jnp.int32)
    in_grp = (jnp.repeat(tile_grp, tm)[:, None] == jnp.arange(N_GROUPS, dtype=jnp.int32)[None, :]).astype(jnp.int32)
    filled = pos - jnp.sum(in_grp * start[None, :], axis=1) < jnp.sum(in_grp * counts[None, :], axis=1)
    dst = jnp.where(filled & (jnp.repeat(tile_live, tm) > 0), src, n + pos % tm)
    return _experts(xn, tile_grp, tile_live, src, dst, l, w, tm=tm)


def _ple_kernel(h_ref, y_ref, p_ref, g_ref, wg_ref, wp_ref, o_ref):
    h = h_ref[...] + _load_row_major(y_ref)
    hn = _rms(h, g_ref[...]).astype(BF16)
    gate = _sigmoid(_dot(hn, wg_ref[...]))
    o_ref[...] = h + _dot(p_ref[...].astype(BF16), wp_ref[...]) * gate


def _ple(h, y, p, l, w, *, tm, y_row0=0):
    y_tile0 = y_row0 // tm
    assert y_tile0 * tm == y_row0
    n = h.shape[0]
    return pl.pallas_call(
        _ple_kernel,
        out_shape=jax.ShapeDtypeStruct(h.shape, F32),
        grid=(n // tm,),
        in_specs=[
            pl.BlockSpec((tm, D_MODEL), lambda i: (i, 0)),
            _row_major_spec(tm, lambda i: (i + y_tile0, 0)),
            pl.BlockSpec((None, tm, PLE_DIM), lambda i: (l, i, 0)),
            pl.BlockSpec((None, 1, D_MODEL), lambda i: (l, 0, 0)),
            pl.BlockSpec((None, D_MODEL, D_MODEL), lambda i: (l, 0, 0)),
            pl.BlockSpec((None, PLE_DIM, D_MODEL), lambda i: (l, 0, 0)),
        ],
        out_specs=pl.BlockSpec((tm, D_MODEL), lambda i: (i, 0)),
        compiler_params=_cparams("parallel"),
        name="ple",
    )(h, y, p, w["norm_ple"], w["w_ple_gate"], w["w_ple"])


def _head_rms(x, g):
    outs = []
    for hh in range(N_HEADS):
        outs.append(_rms(x[:, hh * HEAD_DIM:(hh + 1) * HEAD_DIM], g))
    return jnp.concatenate(outs, axis=-1)


FEAT_BLOCK_START = 16
FEAT_OFFSET = 17
FEAT_ONES_ROWS = 8
VT_ONES_ROWS = 16


def _alibi_slope(hh):
    return 2.0 ** (-ALIBI_MAX_BIAS * (hh + 1) / N_HEADS)


assert all(_alibi_slope(hh) == 2.0 ** round(-ALIBI_MAX_BIAS * (hh + 1) / N_HEADS) for hh in range(N_HEADS))


def _kv_kernel(h_ref, g_ref, w_ref, kn_ref, k_ref, v_ref, *attn_refs, blocks_per_seq):
    hn = _rms(h_ref[...], g_ref[...]).astype(BF16)
    kv = _dot(hn, w_ref[...])
    k = _head_rms(kv[:, :D_MODEL], kn_ref[...])
    v = kv[:, D_MODEL:]
    k_ref[...] = k
    v_ref[...] = v
    if not attn_refs:
        return
    ka_ref, vt_ref, km_ref = attn_refs
    km_ref[0] = jnp.sum(k, axis=0, keepdims=True) / MOBA_BLOCK
    tm = k.shape[0]
    blk = pl.program_id(0) % blocks_per_seq
    lane = lax.broadcasted_iota(jnp.int32, (tm, HEAD_DIM), 1)
    off = lax.broadcasted_iota(jnp.int32, (tm, HEAD_DIM), 0).astype(F32)
    onehot = jnp.where(lane == blk, 1.0, 0.0)
    start = (blk * MOBA_BLOCK).astype(F32)
    for hh in range(N_HEADS):
        slope = _alibi_slope(hh)
        feat = onehot + jnp.where(lane == FEAT_BLOCK_START, slope * start, 0.0) \
            + jnp.where(lane == FEAT_OFFSET, slope * off, 0.0)
        c0 = hh * 2 * HEAD_DIM
        ka_ref[:, c0:c0 + HEAD_DIM] = k[:, hh * HEAD_DIM:(hh + 1) * HEAD_DIM].astype(BF16)
        ka_ref[:, c0 + HEAD_DIM:c0 + 2 * HEAD_DIM] = feat.astype(BF16)
        vt_ref[hh, 0, 0:HEAD_DIM, :] = v[:, hh * HEAD_DIM:(hh + 1) * HEAD_DIM].T.astype(BF16)
        vt_ref[hh, 0, HEAD_DIM:HEAD_DIM + VT_ONES_ROWS, :] = jnp.ones((VT_ONES_ROWS, tm), BF16)


def _shared_kv(h, w, *, tm, blocks_per_seq=None):
    n = h.shape[0]
    tok = pl.BlockSpec((tm, D_MODEL), lambda i: (i, 0))
    out_shape = [jax.ShapeDtypeStruct((n, D_MODEL), F32), jax.ShapeDtypeStruct((n, D_MODEL), F32)]
    out_specs = [tok, tok]
    if blocks_per_seq is not None:
        out_shape += [
            jax.ShapeDtypeStruct((n, 2 * D_MODEL), BF16),
            jax.ShapeDtypeStruct((N_HEADS, n // tm, HEAD_DIM + VT_ONES_ROWS, tm), BF16),
            jax.ShapeDtypeStruct((n // tm, 1, D_MODEL), F32),
        ]
        out_specs += [
            pl.BlockSpec((tm, 2 * D_MODEL), lambda i: (i, 0)),
            pl.BlockSpec((N_HEADS, 1, HEAD_DIM + VT_ONES_ROWS, tm), lambda i: (0, i, 0, 0)),
            pl.BlockSpec((1, 1, D_MODEL), lambda i: (i, 0, 0)),
        ]
    return pl.pallas_call(
        functools.partial(_kv_kernel, blocks_per_seq=blocks_per_seq),
        out_shape=tuple(out_shape),
        grid=(n // tm,),
        in_specs=[
            tok,
            pl.BlockSpec((1, D_MODEL), lambda i: (0, 0)),
            pl.BlockSpec((D_MODEL, 2 * D_MODEL), lambda i: (0, 0)),
            pl.BlockSpec((1, HEAD_DIM), lambda i: (0, 0)),
        ],
        out_specs=tuple(out_specs),
        compiler_params=_cparams("parallel"),
        name="shared_kv",
    )(h, w["norm_kv"], w["w_kv"], w["k_norm"])


def _rank_in_segments(gv, seg):
    n_lanes = gv.shape[-1]
    lane = lax.broadcasted_iota(jnp.int32, gv.shape, 1)
    n = lane & (seg - 1)
    rank = jnp.zeros(gv.shape, jnp.int32)
    for k in range(1, seg):
        ahead = pltpu.roll(gv, n_lanes - k, axis=1)
        behind = pltpu.roll(gv, seg - k, axis=1)
        wrap = n + k >= seg
        partner = jnp.where(wrap, behind, ahead)
        rank = rank + jnp.where(wrap, (partner >= gv).astype(jnp.int32), (partner > gv).astype(jnp.int32))
    return rank


def _queries(h_ref, g_ref, wq_ref, qn_ref):
    hn = _rms(h_ref[...], g_ref[...]).astype(BF16)
    return _head_rms(_dot(hn, wq_ref[...]), qn_ref[...])


def _q_prompt_kernel(h_ref, g_ref, wq_ref, qn_ref, kmh_ref, kml_ref, qt_ref, *, blocks_per_seq):
    own = pl.program_id(0) % blocks_per_seq
    q = _queries(h_ref, g_ref, wq_ref, qn_ref)
    tm = q.shape[0]
    gate = _dot3(q, kmh_ref[0], kml_ref[0])
    lane = lax.broadcasted_iota(jnp.int32, gate.shape, 1)
    blk = lane & (blocks_per_seq - 1)
    valid = blk < own
    gv = jnp.where(valid, gate, -jnp.inf)
    rank = _rank_in_segments(gv, blocks_per_seq)
    keep = jnp.where(valid, (rank < MOBA_TOP_K).astype(jnp.int32), (blk == own).astype(jnp.int32))
    bias_t = jnp.where(keep > 0, 0.0, NEG).T
    ones = jnp.ones((FEAT_ONES_ROWS, tm), F32)
    zeros = jnp.zeros((HEAD_DIM - blocks_per_seq - FEAT_ONES_ROWS, tm), F32)
    for hh in range(N_HEADS):
        qh = q[:, hh * HEAD_DIM:(hh + 1) * HEAD_DIM] * HEAD_DIM ** -0.5
        feat = jnp.concatenate([bias_t[hh * blocks_per_seq:(hh + 1) * blocks_per_seq], ones, zeros], axis=0)
        qt_ref[hh, 0, 0:HEAD_DIM, :] = qh.T.astype(BF16)
        qt_ref[hh, 0, HEAD_DIM:2 * HEAD_DIM, :] = feat.astype(BF16)


def _q_prompt(h, j, w, km_hi, km_lo, *, blocks_per_seq):
    assert blocks_per_seq == FEAT_BLOCK_START and N_HEADS * blocks_per_seq == LANES
    n = h.shape[0]
    tm = MOBA_BLOCK
    kern = functools.partial(_q_prompt_kernel, blocks_per_seq=blocks_per_seq)
    return pl.pallas_call(
        kern,
        out_shape=jax.ShapeDtypeStruct((N_HEADS, n // tm, 2 * HEAD_DIM, tm), BF16),
        grid=(n // tm,),
        in_specs=[
            pl.BlockSpec((tm, D_MODEL), lambda i: (i, 0)),
            pl.BlockSpec((None, 1, D_MODEL), lambda i: (N_A_LAYERS + j, 0, 0)),
            pl.BlockSpec((None, D_MODEL, D_MODEL), lambda i: (j, 0, 0)),
            pl.BlockSpec((None, 1, HEAD_DIM), lambda i: (j, 0, 0)),
            pl.BlockSpec((1, D_MODEL, LANES), lambda i: (i // blocks_per_seq, 0, 0)),
            pl.BlockSpec((1, D_MODEL, LANES), lambda i: (i // blocks_per_seq, 0, 0)),
        ],
        out_specs=pl.BlockSpec((N_HEADS, 1, 2 * HEAD_DIM, tm), lambda i: (0, i, 0, 0)),
        compiler_params=_cparams("parallel"),
        name="q_prompt",
    )(h, w["norm_mix"], w["w_q"], w["q_norm"], km_hi, km_lo)


def _q_sample_kernel(h_ref, g_ref, wq_ref, qn_ref, kmh_ref, kml_ref, q_ref, idx_ref, *, n_blk):
    q = _queries(h_ref, g_ref, wq_ref, qn_ref)
    q_ref[...] = q * HEAD_DIM ** -0.5
    gate = _dot3(q, kmh_ref[0], kml_ref[0])
    rank = _rank_in_segments(gate, n_blk)
    lane = lax.broadcasted_iota(jnp.int32, gate.shape, 1)
    blk = (lane & (n_blk - 1)).astype(F32)
    row = lax.broadcasted_iota(jnp.int32, (gate.shape[1], LANES), 0)
    col = lax.broadcasted_iota(jnp.int32, (gate.shape[1], LANES), 1)
    seg_of = jnp.where((row >= col * n_blk) & (row < (col + 1) * n_blk), 1.0, 0.0).astype(BF16)
    for r in range(MOBA_TOP_K):
        picked = jnp.where(rank == r, blk, 0.0).astype(BF16)
        idx_ref[r] = _dot(picked, seg_of).astype(jnp.int32)


def _q_sample(h, j, w, km_hi, km_lo, *, n_blk):
    n = h.shape[0]
    tm = SAMPLE_ROWS
    kern = functools.partial(_q_sample_kernel, n_blk=n_blk)
    return pl.pallas_call(
        kern,
        out_shape=(jax.ShapeDtypeStruct((n, D_MODEL), F32), jax.ShapeDtypeStruct((MOBA_TOP_K, n, LANES), jnp.int32)),
        grid=(n // tm,),
        in_specs=[
            pl.BlockSpec((tm, D_MODEL), lambda i: (i, 0)),
            pl.BlockSpec((None, 1, D_MODEL), lambda i: (N_A_LAYERS + j, 0, 0)),
            pl.BlockSpec((None, D_MODEL, D_MODEL), lambda i: (j, 0, 0)),
            pl.BlockSpec((None, 1, HEAD_DIM), lambda i: (j, 0, 0)),
            pl.BlockSpec((1, D_MODEL, N_HEADS * n_blk), lambda i: (i, 0, 0)),
            pl.BlockSpec((1, D_MODEL, N_HEADS * n_blk), lambda i: (i, 0, 0)),
        ],
        out_specs=(pl.BlockSpec((tm, D_MODEL), lambda i: (i, 0)), pl.BlockSpec((MOBA_TOP_K, tm, LANES), lambda i: (0, i, 0))),
        compiler_params=_cparams("parallel"),
        name="q_sample",
    )(h, w["norm_mix"], w["w_q"], w["q_norm"], km_hi, km_lo)


def _block_diag_means(km, n_blk):
    bsz = km.shape[0]
    eye = jnp.eye(N_HEADS, dtype=F32)
    t = km.transpose(0, 2, 3, 1)[:, :, :, None, :] * eye[None, :, None, :, None]
    return _split(t.reshape(bsz, D_MODEL, N_HEADS * n_blk))


HEADS_PER_STEP = 4


def _attn_prompt_kernel(qt_ref, ka_ref, vt_ref, o_ref):
    i = pl.program_id(2)
    blk = MOBA_BLOCK
    rows = lax.broadcasted_iota(jnp.int32, (blk, blk), 0)
    cols = lax.broadcasted_iota(jnp.int32, (blk, blk), 1)
    qts = [qt_ref[hp, 0] for hp in range(HEADS_PER_STEP)]

    def scores(hp, n):
        keys = ka_ref[pl.ds(pl.multiple_of(n * blk, blk), blk), hp * 2 * HEAD_DIM:(hp + 1) * 2 * HEAD_DIM]
        return _dot(keys, qts[hp])

    def absorb(s, vt, m, acc):
        m_new = jnp.maximum(m, jnp.max(s, axis=0, keepdims=True))
        alpha = jnp.exp(m - m_new)
        p = jnp.exp(s - m_new)
        acc = alpha * acc + _dot(vt, p.astype(BF16))
        return m_new, acc

    init = tuple(
        (jnp.where(cols >= rows, scores(hp, i), NEG), jnp.full((1, blk), NEG, F32),
         jnp.zeros((HEAD_DIM + VT_ONES_ROWS, blk), F32))
        for hp in range(HEADS_PER_STEP))

    def body(n, carry):
        prev = jnp.where(n == 0, i, n - 1)
        out = []
        for hp in range(HEADS_PER_STEP):
            s, m, acc = carry[hp]
            s_next = scores(hp, n)
            out.append((s_next,) + absorb(s, vt_ref[hp, prev], m, acc))
        return tuple(out)

    carry = lax.fori_loop(0, i, body, init)
    last = jnp.where(i == 0, i, i - 1)
    for hp in range(HEADS_PER_STEP):
        s, m, acc = carry[hp]
        m, acc = absorb(s, vt_ref[hp, last], m, acc)
        out = acc[:HEAD_DIM] / acc[HEAD_DIM:HEAD_DIM + 1]
        o_ref[:, hp * HEAD_DIM:(hp + 1) * HEAD_DIM] = out.T.astype(BF16)


def _attn_prompt(qt, ka, vt, *, bsz, blocks_per_seq):
    n = ka.shape[0]
    seq = blocks_per_seq * MOBA_BLOCK
    hps = HEADS_PER_STEP
    return pl.pallas_call(
        _attn_prompt_kernel,
        out_shape=jax.ShapeDtypeStruct((n, D_MODEL), BF16),
        grid=(bsz, N_HEADS // hps, blocks_per_seq),
        in_specs=[
            pl.BlockSpec((hps, 1, 2 * HEAD_DIM, MOBA_BLOCK), lambda b, h, i: (h, b * blocks_per_seq + i, 0, 0)),
            pl.BlockSpec((seq, hps * 2 * HEAD_DIM), lambda b, h, i: (b, h)),
            pl.BlockSpec((hps, blocks_per_seq, HEAD_DIM + VT_ONES_ROWS, MOBA_BLOCK), lambda b, h, i: (h, b, 0, 0)),
        ],
        out_specs=pl.BlockSpec((MOBA_BLOCK, hps * HEAD_DIM), lambda b, h, i: (b * blocks_per_seq + i, h)),
        compiler_params=_cparams("parallel", "parallel", "arbitrary"),
        name="attn_prompt",
    )(qt, ka, vt)


def _kmean_cache_kernel(pt_ref, *refs):
    page_refs, o_ref = refs[:-1], refs[-1]
    for r in range(len(page_refs) // PAGES_PER_BLOCK):
        tot = jnp.sum(page_refs[2 * r][0], axis=0) + jnp.sum(page_refs[2 * r + 1][0], axis=0)
        o_ref[0, r] = tot / MOBA_BLOCK


def _kmean_cache(cache_k, page_table, *, pages_per_step=8):
    bsz, n_pages = page_table.shape
    steps = n_pages // pages_per_step
    blk_per_step = pages_per_step // PAGES_PER_BLOCK
    pt = page_table.reshape(-1)

    def page_spec(r):
        return pl.BlockSpec((1, PAGE_SIZE, N_HEADS, HEAD_DIM),
                            lambda b, j, pt_ref: (pt_ref[b * n_pages + j * pages_per_step + r], 0, 0, 0))

    out = pl.pallas_call(
        _kmean_cache_kernel,
        out_shape=jax.ShapeDtypeStruct((bsz * steps, blk_per_step, N_HEADS, HEAD_DIM), F32),
        grid_spec=pltpu.PrefetchScalarGridSpec(
            num_scalar_prefetch=1,
            grid=(bsz, steps),
            in_specs=[page_spec(r) for r in range(pages_per_step)],
            out_specs=pl.BlockSpec((1, blk_per_step, N_HEADS, HEAD_DIM), lambda b, j, pt_ref: (b * steps + j, 0, 0, 0)),
        ),
        compiler_params=_cparams("parallel", "arbitrary"),
        name="kmean_cache",
    )(pt, *([cache_k] * pages_per_step))
    return out.reshape(bsz, n_pages // PAGES_PER_BLOCK, N_HEADS, HEAD_DIM)


def _attn_sample_kernel(pt_ref, sel_ref, q_ref, kn_ref, vn_ref, slope_ref, ck_ref, cv_ref, o_ref,
                        kbuf, vbuf, sem, *, n_q, n_pages, past_len):
    b = pl.program_id(0)
    hh = pl.program_id(1)
    step = b * N_HEADS + hh
    n_steps = pl.num_programs(0) * N_HEADS
    cur = step % 2
    sel_base = step * n_q * MOBA_TOP_K

    def copies(step_, buf):
        b_, h_ = step_ // N_HEADS, step_ % N_HEADS
        out = []
        for qi in range(n_q):
            for r in range(MOBA_TOP_K):
                n = sel_ref[step_ * n_q * MOBA_TOP_K + qi * MOBA_TOP_K + r]
                for pg in range(PAGES_PER_BLOCK):
                    page = pt_ref[b_ * n_pages + n * PAGES_PER_BLOCK + pg]
                    dst = pl.ds(pg * PAGE_SIZE, PAGE_SIZE)
                    out.append(pltpu.make_async_copy(ck_ref.at[page, :, h_, :], kbuf.at[buf, qi, r, dst, :], sem.at[buf, 0]))
                    out.append(pltpu.make_async_copy(cv_ref.at[page, :, h_, :], vbuf.at[buf, qi, r, dst, :], sem.at[buf, 1]))
        return out

    @pl.when(step == 0)
    def _():
        for cp in copies(step, cur):
            cp.start()

    @pl.when(step + 1 < n_steps)
    def _():
        for cp in copies(step + 1, 1 - cur):
            cp.start()

    for cp in copies(step, cur):
        cp.wait()

    slope = slope_ref[0][:, 0:1]
    k_new = kn_ref[...]
    v_new = vn_ref[...]
    key_off = lax.broadcasted_iota(jnp.int32, (MOBA_BLOCK, 1), 0)
    new_off = lax.broadcasted_iota(jnp.int32, (SAMPLE_ROWS, 1), 0)
    out_row = lax.broadcasted_iota(jnp.int32, (SAMPLE_ROWS, HEAD_DIM), 0)
    out = jnp.zeros((SAMPLE_ROWS, HEAD_DIM), F32)
    for qi in range(n_q):
        qrow = q_ref[qi:qi + 1, :]
        s_own = jnp.sum(k_new * qrow, axis=-1, keepdims=True) - slope * (qi - new_off).astype(F32)
        s_own = jnp.where(new_off <= qi, s_own, NEG)
        scores = []
        m = jnp.max(s_own, axis=0, keepdims=True)
        for r in range(MOBA_TOP_K):
            n = sel_ref[sel_base + qi * MOBA_TOP_K + r]
            dist = (past_len + qi - n * MOBA_BLOCK - key_off).astype(F32)
            s = jnp.sum(kbuf[cur, qi, r] * qrow, axis=-1, keepdims=True) - slope * dist
            scores.append(s)
            m = jnp.maximum(m, jnp.max(s, axis=0, keepdims=True))
        p_own = jnp.exp(s_own - m)
        l = jnp.sum(p_own, axis=0, keepdims=True)
        acc = jnp.sum(p_own * v_new, axis=0, keepdims=True)
        for r in range(MOBA_TOP_K):
            p = jnp.exp(scores[r] - m)
            l = l + jnp.sum(p, axis=0, keepdims=True)
            acc = acc + jnp.sum(p * vbuf[cur, qi, r], axis=0, keepdims=True)
        out = jnp.where(out_row == qi, acc / l, out)
    o_ref[...] = out.astype(BF16)


def _attn_sample(q, k_new, v_new, sel, cache_k, cache_v, page_table, slopes, *, n_q):
    bsz, n_pages = page_table.shape
    kern = functools.partial(_attn_sample_kernel, n_q=n_q, n_pages=n_pages, past_len=n_pages * PAGE_SIZE)
    tok = pl.BlockSpec((SAMPLE_ROWS, HEAD_DIM), lambda b, h, pt_ref, sel_ref: (b, h))
    return pl.pallas_call(
        kern,
        out_shape=jax.ShapeDtypeStruct(q.shape, BF16),
        grid_spec=pltpu.PrefetchScalarGridSpec(
            num_scalar_prefetch=2,
            grid=(bsz, N_HEADS),
            in_specs=[
                tok, tok, tok,
                pl.BlockSpec((1, 1, LANES), lambda b, h, pt_ref, sel_ref: (h, 0, 0)),
                pl.BlockSpec(memory_space=pl.ANY),
                pl.BlockSpec(memory_space=pl.ANY),
            ],
            out_specs=tok,
            scratch_shapes=[
                pltpu.VMEM((2, n_q, MOBA_TOP_K, MOBA_BLOCK, HEAD_DIM), F32),
                pltpu.VMEM((2, n_q, MOBA_TOP_K, MOBA_BLOCK, HEAD_DIM), F32),
                pltpu.SemaphoreType.DMA((2, 2)),
            ],
        ),
        compiler_params=_cparams("arbitrary", "arbitrary"),
        name="attn_sample",
    )(page_table.reshape(-1), sel, q, k_new, v_new, slopes, cache_k, cache_v)


def _oproj_kernel(h_ref, o_ref, w_ref, out_ref):
    out_ref[...] = h_ref[...] + _dot(o_ref[...], w_ref[...])


def _oproj(h, o, j, w, *, tm):
    n = h.shape[0]
    tok = pl.BlockSpec((tm, D_MODEL), lambda i: (i, 0))
    return pl.pallas_call(
        _oproj_kernel,
        out_shape=jax.ShapeDtypeStruct(h.shape, F32),
        grid=(n // tm,),
        in_specs=[tok, tok, pl.BlockSpec((None, D_MODEL, D_MODEL), lambda i: (j, 0, 0))],
        out_specs=tok,
        compiler_params=_cparams("parallel"),
        name="oproj",
    )(h, o, w["w_o"])


def _prepare_weights(norm_mix, norm_ffn, norm_ple, w_pool, pool_scale, norm_kv, w_kv, k_norm, w_q, q_norm, w_o,
                     router_group_w, router_group_b, router_expert_w, router_expert_b, w_gate, w_up, w_down,
                     w_ple, w_ple_gate):
    pad = LANES - N_GROUPS - N_EXPERTS
    wr = jnp.concatenate([router_group_w, router_expert_w, jnp.zeros((DEPTH, D_MODEL, pad), F32)], axis=-1)
    br = jnp.concatenate([router_group_b, router_expert_b, jnp.zeros((DEPTH, pad), F32)], axis=-1)
    wr_hi, wr_lo = _split(wr)
    wp_hi, wp_lo = _split(w_pool)
    return dict(
        norm_mix=norm_mix.reshape(DEPTH, 1, D_MODEL), norm_ffn=norm_ffn.reshape(DEPTH, 1, D_MODEL),
        norm_ple=norm_ple.reshape(DEPTH, 1, D_MODEL), wp_hi=wp_hi, wp_lo=wp_lo,
        pool_scale=pool_scale.reshape(N_A_LAYERS, 1, D_MODEL), norm_kv=norm_kv.reshape(1, D_MODEL),
        w_kv=w_kv.astype(BF16), k_norm=k_norm.reshape(1, HEAD_DIM), w_q=w_q.astype(BF16),
        q_norm=q_norm.reshape(-1, 1, HEAD_DIM), w_o=w_o.astype(BF16),
        wr_hi=wr_hi, wr_lo=wr_lo, br=br.reshape(DEPTH, 1, LANES),
        w_gate=w_gate.astype(BF16), w_up=w_up.astype(BF16), w_down=w_down.astype(BF16),
        w_ple=w_ple.astype(BF16), w_ple_gate=w_ple_gate.astype(BF16),
    )


def kernel(x_prompt, x_sample, p_prompt, p_sample, state_pool, cache_k, cache_v, page_table, norm_mix, norm_ffn, norm_ple, w_pool, pool_scale, norm_kv, w_kv, k_norm, w_q, q_norm, w_o, router_group_w, router_group_b, router_expert_w, router_expert_b, w_gate, w_up, w_down, w_ple, w_ple_gate):
    w = _prepare_weights(norm_mix, norm_ffn, norm_ple, w_pool, pool_scale, norm_kv, w_kv, k_norm, w_q, q_norm, w_o,
                         router_group_w, router_group_b, router_expert_w, router_expert_b, w_gate, w_up, w_down,
                         w_ple, w_ple_gate)
    slopes = jnp.exp2(-ALIBI_MAX_BIAS * jnp.arange(1, N_HEADS + 1, dtype=F32) / N_HEADS)
    slopes = jnp.broadcast_to(slopes[:, None, None], (N_HEADS, 1, LANES))

    bsz, seq, _ = x_prompt.shape
    dec_b, dec_q, _ = x_sample.shape
    n_pages = page_table.shape[1]
    past_len = n_pages * PAGE_SIZE
    blocks_per_seq = seq // MOBA_BLOCK
    n_past_blk = past_len // MOBA_BLOCK
    n_p, n_s = bsz * seq, dec_b * SAMPLE_ROWS
    row_pad = ((0, 0), (0, 0), (0, SAMPLE_ROWS - dec_q), (0, 0))

    hp = x_prompt
    hs = jnp.pad(x_sample, row_pad[1:])
    pp = p_prompt.reshape(DEPTH, n_p, PLE_DIM)
    ps = jnp.pad(p_sample, row_pad).reshape(DEPTH, n_s, PLE_DIM)
    pre_p = jnp.zeros((bsz, POOL_HALO, D_MODEL), F32)
    pre_s = jnp.pad(state_pool, ((0, 0), (0, 0), (1, 0), (0, 0)))

    pool_p, pool_s = [], []
    kv_p = kv_s = km_p = km_s = None
    for l in range(DEPTH):
        if l < N_A_LAYERS:
            g, sc = w["norm_mix"][l], w["pool_scale"][l]
            hp, st = _pool_mixer(hp.reshape(bsz, seq, D_MODEL), pre_p, g, w["wp_hi"][l], w["wp_lo"][l], sc,
                                 ts=512, n_valid=512, pos0=0)
            pool_p.append(st[:, 1:])
            hs, st = _pool_mixer(hs.reshape(dec_b, SAMPLE_ROWS, D_MODEL), pre_s[l], g, w["wp_hi"][l], w["wp_lo"][l], sc,
                                 ts=SAMPLE_ROWS, n_valid=dec_q, pos0=past_len)
            pool_s.append(st[:, 1:])
            hp, hs = hp.reshape(n_p, D_MODEL), hs.reshape(n_s, D_MODEL)
        else:
            j = l - N_A_LAYERS
            qt = _q_prompt(hp, j, w, *km_p, blocks_per_seq=blocks_per_seq)
            o = _attn_prompt(qt, kv_p[2], kv_p[3], bsz=bsz, blocks_per_seq=blocks_per_seq)
            hp = _oproj(hp, o, j, w, tm=512)
            q, idx = _q_sample(hs, j, w, *km_s, n_blk=n_past_blk)
            sel = idx.reshape(MOBA_TOP_K, dec_b, SAMPLE_ROWS, LANES)[:, :, :dec_q, :N_HEADS]
            sel = sel.transpose(1, 3, 2, 0).reshape(-1)
            o = _attn_sample(q, kv_s[0], kv_s[1], sel, cache_k, cache_v, page_table, slopes, n_q=dec_q)
            hs = _oproj(hs, o, j, w, tm=n_s)
        y = _moe(hp, hs, l, w)
        hp = _ple(hp, y, pp, l, w, tm=512)
        hs = _ple(hs, y, ps, l, w, tm=n_s, y_row0=n_p)
        if l == N_A_LAYERS - 1:
            kv_p = _shared_kv(hp, w, tm=MOBA_BLOCK, blocks_per_seq=blocks_per_seq)
            kv_s = _shared_kv(hs, w, tm=n_s)
            km_p = _block_diag_means(kv_p[4].reshape(bsz, blocks_per_seq, N_HEADS, HEAD_DIM), blocks_per_seq)
            km_s = _block_diag_means(_kmean_cache(cache_k, page_table), n_past_blk)

    heads = (N_HEADS, HEAD_DIM)
    y_prompt = hp.reshape(bsz, seq, D_MODEL)
    y_sample = hs.reshape(dec_b, SAMPLE_ROWS, D_MODEL)[:, :dec_q]
    k_prompt, v_prompt = (t.reshape(bsz, seq, *heads) for t in kv_p[:2])
    k_sample, v_sample = (t.reshape(dec_b, SAMPLE_ROWS, *heads)[:, :dec_q] for t in kv_s[:2])
    return (y_prompt, y_sample, jnp.stack(pool_p), jnp.stack(pool_s), k_prompt, v_prompt, k_sample, v_sample)
```
